```python
import jax, jax.numpy as jnp
from jax import lax
import numpy as np

D_MODEL = 1024
BATCH = 4
SEQ = 8192
DEPTH = 2

CHUNK = 64
MEM_LEN = 256
EPS = 1e-6
NEG_INF = -1e30

A_Q_HEADS = 8
A_KV_HEADS = 2
A_HEAD_DIM = 64
A_WINDOW = 128
A_WIN_CHUNKS = A_WINDOW // CHUNK
A_WIDTH = A_Q_HEADS * A_HEAD_DIM

B_HEADS = 4
B_KEY_DIM = 128
B_VAL_DIM = 128
B_WIDTH = B_HEADS * B_VAL_DIM
ROT_BASE = 10000.0

C_WIDTH = D_MODEL
C_CONV = 3

X_HEADS = 4
X_HEAD_DIM = D_MODEL // X_HEADS

EVEN_SPLITS = (A_WIDTH,
               A_KV_HEADS * A_HEAD_DIM,
               A_KV_HEADS * A_HEAD_DIM,
               A_WIDTH,
               B_HEADS * B_KEY_DIM,
               B_HEADS * B_KEY_DIM,
               B_WIDTH,
               B_WIDTH)
EVEN_IN = sum(EVEN_SPLITS)
EVEN_SPLIT_POINTS = tuple(int(p) for p in np.cumsum(EVEN_SPLITS)[:-1])
MIX_WIDTH = A_WIDTH + B_WIDTH
ODD_IN = 4 * C_WIDTH
N_EVEN = (DEPTH + 1) // 2
N_ODD = DEPTH // 2

kernel_name = "hybrid_swa_retention_shortconv_memxattn"


def rmsnorm(x, g):
    xf = x.astype(jnp.float32)
    y = xf * lax.rsqrt(jnp.mean(xf * xf, axis=-1, keepdims=True) + EPS)
    return (y * g.astype(jnp.float32)).astype(x.dtype)


def swa_sink_attention(q, k, v, sink):
    b, s, _, d = q.shape
    n = s // CHUNK
    g = A_Q_HEADS // A_KV_HEADS
    nband = (A_WIN_CHUNKS + 1) * CHUNK
    qc = q.reshape(b, n, CHUNK, A_KV_HEADS, g, d)

    def band(t):
        tc = t.reshape(b, n, CHUNK, A_KV_HEADS, d)
        tp = jnp.pad(tc, ((0, 0), (A_WIN_CHUNKS, 0), (0, 0), (0, 0), (0, 0)))
        return jnp.concatenate([tp[:, j:j + n] for j in range(A_WIN_CHUNKS + 1)], axis=2)

    kb, vb = band(k), band(v)
    key_chunk = jnp.arange(n)[:, None] - A_WIN_CHUNKS + jnp.arange(nband)[None, :] // CHUNK
    valid = key_chunk >= 0
    scores = jnp.einsum('bnqhgd,bnkhd->bnhgqk', qc, kb).astype(jnp.float32) * (d ** -0.5)
    scores = jnp.where(valid[None, :, None, None, None, :], scores, NEG_INF)
    sink_logit = sink.astype(jnp.float32).reshape(1, 1, A_KV_HEADS, g, 1, 1)
    m = jnp.maximum(jnp.max(scores, axis=-1, keepdims=True), sink_logit)
    p = jnp.exp(scores - m)
    denom = jnp.sum(p, axis=-1, keepdims=True) + jnp.exp(sink_logit - m)
    p = (p / denom).astype(v.dtype)
    out = jnp.einsum('bnhgqk,bnkhd->bnqhgd', p, vb)
    return out.reshape(b, s, A_Q_HEADS * d)


def rotary(t, pos):
    d = t.shape[-1]
    inv = 1.0 / (ROT_BASE ** jnp.linspace(0.0, 1.0, d // 2, dtype=jnp.float32))
    ang = pos.astype(jnp.float32)[:, None] * inv[None, :]
    cos = jnp.cos(ang)[None, :, None, :].astype(t.dtype)
    sin = jnp.sin(ang)[None, :, None, :].astype(t.dtype)
    t1, t2 = t[..., :d // 2], t[..., d // 2:]
    return jnp.concatenate([t1 * cos - t2 * sin, t1 * sin + t2 * cos], axis=-1)


def retention(q, k, v):
    b, s, h, dk = q.shape
    dv = v.shape[-1]
    n = s // CHUNK
    dt = q.dtype
    pos = jnp.arange(s)
    q = rotary(q, pos)
    k = rotary(k, pos) * (dk ** -0.5)
    log_gamma = jnp.log1p(-jnp.exp2(-5.0 - jnp.arange(h, dtype=jnp.float32)))
    idx = jnp.arange(CHUNK, dtype=jnp.float32)
    intra = jnp.exp(log_gamma[:, None, None] * jnp.abs(idx[:, None] - idx[None, :]))
    k_decay = jnp.exp(log_gamma[:, None] * (CHUNK - 1 - idx)[None, :])
    q_decay = jnp.exp(log_gamma[:, None] * (idx + 1.0)[None, :])
    chunk_decay = jnp.exp(log_gamma * CHUNK).astype(dt)[None, :, None, None]

    qc = q.reshape(b, n, CHUNK, h, dk)
    kc = k.reshape(b, n, CHUNK, h, dk)
    vc = v.reshape(b, n, CHUNK, h, dv)
    sc = jnp.einsum('bnihd,bnjhd->bnhij', qc, kc) * intra.astype(dt)[None, None]
    o_intra = jnp.einsum('bnhij,bnjhe->bnihe', sc, vc)
    kv = jnp.einsum('bnjhd,bnjhe,hj->nbhde', kc, vc, k_decay.astype(dt))

    def step(state, kv_c):
        return state * chunk_decay + kv_c, state

    _, states = lax.scan(step, jnp.zeros((b, h, dk, dv), dt), kv)
    o_cross = jnp.einsum('bnihd,nbhde,hi->bnihe', qc, states, q_decay.astype(dt))
    o = (o_intra + o_cross).astype(jnp.float32)
    o = o * lax.rsqrt(jnp.mean(o * o, axis=-1, keepdims=True) + EPS)
    return o.astype(dt).reshape(b, s, h * dv)


def even_mixer(xn, w_in, sink, w_out):
    b, s, _ = xn.shape
    hcat = xn @ w_in
    aq, ak, av, ag, bq, bk, bv, bg = jnp.split(hcat, EVEN_SPLIT_POINTS, axis=-1)
    ya = swa_sink_attention(aq.reshape(b, s, A_Q_HEADS, A_HEAD_DIM),
                            ak.reshape(b, s, A_KV_HEADS, A_HEAD_DIM),
                            av.reshape(b, s, A_KV_HEADS, A_HEAD_DIM), sink)
    yb = retention(bq.reshape(b, s, B_HEADS, B_KEY_DIM),
                   bk.reshape(b, s, B_HEADS, B_KEY_DIM),
                   bv.reshape(b, s, B_HEADS, B_VAL_DIM))
    y = jnp.concatenate([ya * jax.nn.silu(ag), yb * jax.nn.silu(bg)], axis=-1)
    return y @ w_out


def conv_mixer(xn, w_in, conv_w, conv_b, w_out):
    s = xn.shape[1]
    gb, gc, u, gate = jnp.split(xn @ w_in, 4, axis=-1)
    z = gc * u
    zp = jnp.pad(z, ((0, 0), (C_CONV - 1, 0), (0, 0)))
    conv = conv_b
    for j in range(C_CONV):
        conv = conv + zp[:, j:j + s] * conv_w[j]
    y = gb * conv * jax.nn.silu(gate)
    return y @ w_out


def memory_cross_attention(xn, memn, wq, wkv, wo):
    b, s, _ = xn.shape
    q = (xn @ wq).reshape(b, s, X_HEADS, X_HEAD_DIM)
    k, v = jnp.split(memn @ wkv, 2, axis=-1)
    k = k.reshape(b, -1, X_HEADS, X_HEAD_DIM)
    v = v.reshape(b, -1, X_HEADS, X_HEAD_DIM)
    sc = jnp.einsum('bshd,bmhd->bhsm', q, k).astype(jnp.float32) * (X_HEAD_DIM ** -0.5)
    p = jax.nn.softmax(sc, axis=-1).astype(v.dtype)
    o = jnp.einsum('bhsm,bmhd->bshd', p, v).reshape(b, s, D_MODEL)
    return o @ wo


def setup_inputs(seed: int = 0) -> dict:
    key = jax.random.key(seed)
    ks = jax.random.split(key, 20)
    f32 = jnp.float32

    def w(k, shape, fan_in):
        return jax.random.normal(k, shape, f32) * (fan_in ** -0.5)

    def gain(k, shape):
        return 1.0 + 0.01 * jax.random.normal(k, shape, f32)

    return {
        "x": jax.random.normal(ks[0], (BATCH, SEQ, D_MODEL), f32),
        "mem": jax.random.normal(ks[1], (BATCH, MEM_LEN, D_MODEL), f32),
        "e_norm": gain(ks[2], (N_EVEN, D_MODEL)),
        "e_w_in": w(ks[3], (N_EVEN, D_MODEL, EVEN_IN), D_MODEL),
        "e_sink": 0.5 * jax.random.normal(ks[4], (N_EVEN, A_Q_HEADS), f32),
        "e_w_out": w(ks[5], (N_EVEN, MIX_WIDTH, D_MODEL), MIX_WIDTH),
        "o_norm": gain(ks[6], (N_ODD, D_MODEL)),
        "o_w_in": w(ks[7], (N_ODD, D_MODEL, ODD_IN), D_MODEL),
        "o_conv_w": w(ks[8], (N_ODD, C_CONV, C_WIDTH), C_CONV),
        "o_conv_b": 0.01 * jax.random.normal(ks[9], (N_ODD, C_WIDTH), f32),
        "o_w_out": w(ks[10], (N_ODD, C_WIDTH, D_MODEL), C_WIDTH),
        "c_norm": gain(ks[11], (DEPTH, D_MODEL)),
        "c_mem_norm": gain(ks[12], (DEPTH, D_MODEL)),
        "c_wq": w(ks[13], (DEPTH, D_MODEL, D_MODEL), D_MODEL),
        "c_wkv": w(ks[14], (DEPTH, D_MODEL, 2 * D_MODEL), D_MODEL),
        "c_wo": w(ks[15], (DEPTH, D_MODEL, D_MODEL), D_MODEL),
        "final_norm": gain(ks[16], (D_MODEL,)),
    }


def reference(x, mem, e_norm, e_w_in, e_sink, e_w_out, o_norm, o_w_in, o_conv_w, o_conv_b,
              o_w_out, c_norm, c_mem_norm, c_wq, c_wkv, c_wo, final_norm):
    for i in range(DEPTH):
        j = i // 2
        if i % 2 == 0:
            x = x + even_mixer(rmsnorm(x, e_norm[j]), e_w_in[j], e_sink[j], e_w_out[j])
        else:
            x = x + conv_mixer(rmsnorm(x, o_norm[j]), o_w_in[j], o_conv_w[j], o_conv_b[j], o_w_out[j])
        x = x + memory_cross_attention(rmsnorm(x, c_norm[i]), rmsnorm(mem, c_mem_norm[i]),
                                       c_wq[i], c_wkv[i], c_wo[i])
    return rmsnorm(x, final_norm)
```

```python
import functools

import jax
import jax.numpy as jnp
import numpy as np
from jax import lax
from jax.experimental import pallas as pl
from jax.experimental.pallas import tpu as pltpu

F32 = jnp.float32
BF16 = jnp.bfloat16

D_MODEL = 1024
CHUNK = 64
MEM_LEN = 256
EPS = 1e-6
NEG_INF = -1e30

A_Q_HEADS = 8
A_KV_HEADS = 2
A_HEAD_DIM = 64
A_WIN_CHUNKS = 2
A_WIDTH = A_Q_HEADS * A_HEAD_DIM
A_KV_WIDTH = A_KV_HEADS * A_HEAD_DIM
A_BAND = (A_WIN_CHUNKS + 1) * CHUNK
A_TAIL = A_WIN_CHUNKS * CHUNK

B_HEADS = 4
B_DIM = 128
B_WIDTH = B_HEADS * B_DIM
ROT_BASE = 10000.0
B_GROUP = 256

C_CONV = 3
C_PAD = 8

X_HEADS = 4
X_DIM = D_MODEL // X_HEADS

OFF_AQ = 0
OFF_AK = OFF_AQ + A_WIDTH
OFF_AV = OFF_AK + A_KV_WIDTH
OFF_AG = OFF_AV + A_KV_WIDTH
OFF_BQ = OFF_AG + A_WIDTH
OFF_BK = OFF_BQ + B_WIDTH
OFF_BV = OFF_BK + B_WIDTH
OFF_BG = OFF_BV + B_WIDTH
EVEN_IN = OFF_BG + B_WIDTH
MIX_WIDTH = A_WIDTH + B_WIDTH

TS = 512
VMEM_LIMIT_BYTES = 56 * 1024 * 1024


def _rmsnorm(x, g):
    return x * lax.rsqrt(jnp.mean(x * x, axis=-1, keepdims=True) + EPS) * g


def _silu(x):
    return x * jax.nn.sigmoid(x)


def _dot(a, b):
    return jnp.dot(a, b, preferred_element_type=F32)


def _dot_nt(a, b):
    return lax.dot_general(a, b, (((1,), (1,)), ((), ())), preferred_element_type=F32)


def _memkv_kernel(mem_ref, g_ref, w_ref, kt_ref, v_ref):
    mn = _rmsnorm(mem_ref[0], g_ref[0]).astype(BF16)
    kv = _dot(mn, w_ref[0])
    kt_ref[0, 0] = kv[:, :D_MODEL].T.astype(BF16)
    v_ref[0, 0] = kv[:, D_MODEL:].astype(BF16)


def _cross_attention(x1, cnorm_ref, wq_ref, wo_ref, mkt_ref, mv_ref):
    xn = _rmsnorm(x1, cnorm_ref[...]).astype(BF16)
    q = _dot(xn, wq_ref[...]).astype(BF16)
    outs = []
    for h in range(X_HEADS):
        cs = slice(h * X_DIM, (h + 1) * X_DIM)
        s = _dot(q[:, cs], mkt_ref[0, cs, :])
        m = jnp.max(s, axis=-1, keepdims=True)
        p = jnp.exp(s - m)
        l = jnp.sum(p, axis=-1, keepdims=True)
        o = _dot(p.astype(BF16), mv_ref[0, :, cs]) / l
        outs.append(o.astype(BF16))
    o = jnp.concatenate(outs, axis=1)
    return x1 + _dot(o, wo_ref[...])


def _layer0_kernel(x_ref, cos_ref, sin_ref, dmat_ref, qdec_ref, kdec_ref, gdec_ref, sink_ref,
                   enorm_ref, win_ref, wout_ref, cnorm_ref, wq_ref, wo_ref, mkt_ref, mv_ref,
                   o_ref, h_ref, y_ref, k_ref, ksw_ref, v_ref, vsw_ref, state_ref):
    t = pl.program_id(1)

    @pl.when(t == 0)
    def _():
        zk = jnp.zeros((A_TAIL, A_KV_WIDTH), BF16)
        k_ref[0:A_TAIL, :] = zk
        ksw_ref[0:A_TAIL, :] = zk
        v_ref[0:A_TAIL, 0:A_KV_WIDTH] = zk
        vsw_ref[0:A_TAIL, 0:A_KV_WIDTH] = zk
        ones = jnp.ones((A_TAIL + TS, A_KV_WIDTH), BF16)
        v_ref[:, A_KV_WIDTH:] = ones
        vsw_ref[:, A_KV_WIDTH:] = ones
        state_ref[...] = jnp.zeros_like(state_ref)

    @pl.when(t > 0)
    def _():
        k_ref[0:A_TAIL, :] = k_ref[TS:TS + A_TAIL, :]
        ksw_ref[0:A_TAIL, :] = ksw_ref[TS:TS + A_TAIL, :]
        v_ref[0:A_TAIL, 0:A_KV_WIDTH] = v_ref[TS:TS + A_TAIL, 0:A_KV_WIDTH]
        vsw_ref[0:A_TAIL, 0:A_KV_WIDTH] = vsw_ref[TS:TS + A_TAIL, 0:A_KV_WIDTH]

    x = x_ref[0]
    xn = _rmsnorm(x, enorm_ref[...]).astype(BF16)
    h_ref[...] = _dot(xn, win_ref[...])

    kk = h_ref[:, OFF_AK:OFF_AK + A_KV_WIDTH]
    vv = h_ref[:, OFF_AV:OFF_AV + A_KV_WIDTH]
    k_ref[A_TAIL:, :] = kk.astype(BF16)
    ksw_ref[A_TAIL:, :] = pltpu.roll(kk, A_HEAD_DIM, 1).astype(BF16)
    v_ref[A_TAIL:, 0:A_KV_WIDTH] = vv.astype(BF16)
    vsw_ref[A_TAIL:, 0:A_KV_WIDTH] = pltpu.roll(vv, A_HEAD_DIM, 1).astype(BF16)

    lane = lax.broadcasted_iota(jnp.int32, (CHUNK, 2 * A_HEAD_DIM), 1)
    lo = lane < A_HEAD_DIM
    kidx = lax.broadcasted_iota(jnp.int32, (4 * CHUNK, A_BAND), 1)

    def attend(lhs, kband, vband, sink_col, n_masked):
        s = _dot_nt(lhs, kband)
        if n_masked is not None:
            s = jnp.where(kidx < n_masked, NEG_INF, s)
        m = jnp.maximum(jnp.max(s, axis=-1, keepdims=True), sink_col)
        p = jnp.exp(s - m).astype(BF16)
        pv = _dot(p, vband)
        denom = pv[:, A_KV_WIDTH:] + jnp.exp(sink_col - m)
        return pv[:, :A_KV_WIDTH] / denom

    for c in range(TS // CHUNK):
        r0 = c * CHUNK
        q = h_ref[r0:r0 + CHUNK, OFF_AQ:OFF_AQ + A_WIDTH]
        qc = [q[:, j * 128:(j + 1) * 128] for j in range(4)]
        zero = jnp.zeros_like(qc[0])
        lhs_a = jnp.concatenate([jnp.where(lo, qc[0], zero), jnp.where(lo, qc[1], zero),
                                 jnp.where(lo, zero, qc[2]), jnp.where(lo, zero, qc[3])],
                                axis=0).astype(BF16)
        lhs_b = jnp.concatenate([jnp.where(lo, zero, qc[0]), jnp.where(lo, zero, qc[1]),
                                 jnp.where(lo, qc[2], zero), jnp.where(lo, qc[3], zero)],
                                axis=0).astype(BF16)
        n_masked = None
        if c < A_WIN_CHUNKS:
            n_masked = jnp.where(t == 0, (A_WIN_CHUNKS - c) * CHUNK, 0)
        band = slice(r0, r0 + A_BAND)
        oa = attend(lhs_a, k_ref[band, :], v_ref[band, :], sink_ref[0], n_masked)
        ob = attend(lhs_b, ksw_ref[band, :], vsw_ref[band, :], sink_ref[1], n_masked)
        ya = jnp.concatenate([
            jnp.where(lo, oa[0:64], ob[0:64]),
            jnp.where(lo, oa[64:128], ob[64:128]),
            jnp.where(lo, ob[128:192], oa[128:192]),
            jnp.where(lo, ob[192:256], oa[192:256])], axis=1)
        ag = h_ref[r0:r0 + CHUNK, OFF_AG:OFF_AG + A_WIDTH]
        y_ref[r0:r0 + CHUNK, 0:A_WIDTH] = (ya * _silu(ag)).astype(BF16)

    for g in range(TS // B_GROUP):
        rows = slice(g * B_GROUP, (g + 1) * B_GROUP)
        cos = cos_ref[rows, :]
        sin = sin_ref[rows, :]
        for h in range(B_HEADS):
            cs = slice(h * B_DIM, (h + 1) * B_DIM)
            q = h_ref[rows, OFF_BQ + h * B_DIM:OFF_BQ + (h + 1) * B_DIM]
            k = h_ref[rows, OFF_BK + h * B_DIM:OFF_BK + (h + 1) * B_DIM]
            v = h_ref[rows, OFF_BV + h * B_DIM:OFF_BV + (h + 1) * B_DIM].astype(BF16)
            qr = q * cos + pltpu.roll(q, B_DIM // 2, 1) * sin
            kr = k * cos + pltpu.roll(k, B_DIM // 2, 1) * sin
            sc = _dot_nt(qr.astype(BF16), kr.astype(BF16)) * dmat_ref[h]
            state = state_ref[h]
            lhs = jnp.concatenate([sc.astype(BF16), (qr * qdec_ref[h]).astype(BF16)], axis=1)
            rhs = jnp.concatenate([v, state.astype(BF16)], axis=0)
            o = _dot(lhs, rhs)
            o = o * lax.rsqrt(jnp.mean(o * o, axis=-1, keepdims=True) + EPS)
            kd = (kr * kdec_ref[h]).T.astype(BF16)
            state_ref[h] = state * gdec_ref[h] + _dot(kd, v)
            bg = h_ref[rows, OFF_BG + h * B_DIM:OFF_BG + (h + 1) * B_DIM]
            y_ref[rows, A_WIDTH + h * B_DIM:A_WIDTH + (h + 1) * B_DIM] = (o * _silu(bg)).astype(BF16)

    x1 = x + _dot(y_ref[...], wout_ref[...])
    o_ref[0] = _cross_attention(x1, cnorm_ref, wq_ref, wo_ref, mkt_ref, mv_ref)


def _layer1_kernel(x_ref, onorm_ref, win_ref, convw_ref, convb_ref, wout_ref,
                   cnorm_ref, wq_ref, wo_ref, mkt_ref, mv_ref, fnorm_ref,
                   o_ref, h_ref, z_ref):
    t = pl.program_id(1)

    @pl.when(t == 0)
    def _():
        z_ref[0:C_PAD, :] = jnp.zeros((C_PAD, D_MODEL), F32)

    @pl.when(t > 0)
    def _():
        z_ref[0:C_PAD, :] = z_ref[TS:TS + C_PAD, :]

    x = x_ref[0]
    xn = _rmsnorm(x, onorm_ref[...]).astype(BF16)
    h_ref[...] = _dot(xn, win_ref[...])
    z = h_ref[:, D_MODEL:2 * D_MODEL] * h_ref[:, 2 * D_MODEL:3 * D_MODEL]
    z_ref[C_PAD:, :] = z
    conv = convb_ref[...] + z * convw_ref[2:3, :]
    conv = conv + z_ref[C_PAD - 1:C_PAD - 1 + TS, :] * convw_ref[1:2, :]
    conv = conv + z_ref[C_PAD - 2:C_PAD - 2 + TS, :] * convw_ref[0:1, :]
    y = h_ref[:, 0:D_MODEL] * conv * _silu(h_ref[:, 3 * D_MODEL:4 * D_MODEL])
    x1 = x + _dot(y.astype(BF16), wout_ref[...])
    x2 = _cross_attention(x1, cnorm_ref, wq_ref, wo_ref, mkt_ref, mv_ref)
    o_ref[0] = _rmsnorm(x2, fnorm_ref[...])


def _const_spec(shape):
    return pl.BlockSpec(shape, lambda *_: (0,) * len(shape), pipeline_mode=pl.Buffered(1))


def _retention_tables(seq):
    half = B_DIM // 2
    inv = 1.0 / (ROT_BASE ** jnp.linspace(0.0, 1.0, half, dtype=F32))
    ang = jnp.arange(seq).astype(F32)[:, None] * inv[None, :]
    cos, sin = jnp.cos(ang), jnp.sin(ang)
    cos2 = jnp.concatenate([cos, cos], axis=1)
    sin2 = jnp.concatenate([-sin, sin], axis=1)
    log_gamma = jnp.log1p(-jnp.exp2(-5.0 - jnp.arange(B_HEADS, dtype=F32)))
    idx = jnp.arange(B_GROUP, dtype=F32)
    chunk_of = jnp.arange(B_GROUP) // CHUNK
    visible = (chunk_of[None, :] <= chunk_of[:, None]).astype(F32)
    scale = B_DIM ** -0.5
    dmat = jnp.exp(log_gamma[:, None, None] * jnp.abs(idx[:, None] - idx[None, :])) * visible * scale
    qdec = jnp.exp(log_gamma[:, None] * (idx + 1.0)[None, :])[:, :, None]
    kdec = (jnp.exp(log_gamma[:, None] * (B_GROUP - 1 - idx)[None, :]) * scale)[:, :, None]
    gdec = jnp.broadcast_to(jnp.exp(log_gamma * B_GROUP)[:, None, None], (B_HEADS, 1, B_DIM))
    return cos2, sin2, dmat, qdec, kdec, gdec


def _forward(x, mem, e_norm, e_w_in, e_sink, e_w_out, o_norm, o_w_in, o_conv_w, o_conv_b,
             o_w_out, c_norm, c_mem_norm, c_wq, c_wkv, c_wo, final_norm):
    batch, seq, d = x.shape
    depth = c_wq.shape[0]
    assert d == D_MODEL and depth == 2 and seq % TS == 0 and mem.shape == (batch, MEM_LEN, D_MODEL)
    n_t = seq // TS
    params = pltpu.CompilerParams(dimension_semantics=("arbitrary", "arbitrary"),
                                  vmem_limit_bytes=VMEM_LIMIT_BYTES)

    mkt, mv = pl.pallas_call(
        _memkv_kernel,
        grid=(depth, batch),
        in_specs=[pl.BlockSpec((1, MEM_LEN, D_MODEL), lambda i, b: (b, 0, 0)),
                  pl.BlockSpec((1, 1, D_MODEL), lambda i, b: (i, 0, 0)),
                  pl.BlockSpec((1, D_MODEL, 2 * D_MODEL), lambda i, b: (i, 0, 0))],
        out_specs=[pl.BlockSpec((1, 1, D_MODEL, MEM_LEN), lambda i, b: (i, b, 0, 0)),
                   pl.BlockSpec((1, 1, MEM_LEN, D_MODEL), lambda i, b: (i, b, 0, 0))],
        out_shape=[jax.ShapeDtypeStruct((depth, batch, D_MODEL, MEM_LEN), BF16),
                   jax.ShapeDtypeStruct((depth, batch, MEM_LEN, D_MODEL), BF16)],
        compiler_params=params,
        name="memkv",
    )(mem, c_mem_norm.reshape(depth, 1, D_MODEL), c_wkv.astype(BF16))

    wq = (c_wq * (X_DIM ** -0.5)).astype(BF16)
    wo = c_wo.astype(BF16)

    col_scale = jnp.concatenate([jnp.full((A_WIDTH,), A_HEAD_DIM ** -0.5, F32),
                                 jnp.ones((EVEN_IN - A_WIDTH,), F32)])
    w_in0 = (e_w_in[0] * col_scale[None, :]).astype(BF16)
    cos2, sin2, dmat, qdec, kdec, gdec = _retention_tables(seq)
    sink = e_sink[0].astype(F32)
    sink_cols = jnp.stack([jnp.repeat(sink[jnp.array([0, 2, 5, 7])], CHUNK),
                           jnp.repeat(sink[jnp.array([1, 3, 4, 6])], CHUNK)])[:, :, None]

    tile_spec = pl.BlockSpec((1, TS, D_MODEL), lambda b, t: (b, t, 0))
    mkt_spec = lambda i: pl.BlockSpec((None, 1, D_MODEL, MEM_LEN), lambda b, t: (i, b, 0, 0))
    mv_spec = lambda i: pl.BlockSpec((None, 1, MEM_LEN, D_MODEL), lambda b, t: (i, b, 0, 0))

    x1 = pl.pallas_call(
        _layer0_kernel,
        grid=(batch, n_t),
        in_specs=[tile_spec,
                  pl.BlockSpec((TS, B_DIM), lambda b, t: (t, 0)),
                  pl.BlockSpec((TS, B_DIM), lambda b, t: (t, 0)),
                  _const_spec((B_HEADS, B_GROUP, B_GROUP)),
                  _const_spec((B_HEADS, B_GROUP, 1)),
                  _const_spec((B_HEADS, B_GROUP, 1)),
                  _const_spec((B_HEADS, 1, B_DIM)),
                  _const_spec((2, 4 * CHUNK, 1)),
                  _const_spec((1, D_MODEL)),
                  _const_spec((D_MODEL, EVEN_IN)),
                  _const_spec((MIX_WIDTH, D_MODEL)),
                  _const_spec((1, D_MODEL)),
                  _const_spec((D_MODEL, D_MODEL)),
                  _const_spec((D_MODEL, D_MODEL)),
                  mkt_spec(0), mv_spec(0)],
        out_specs=tile_spec,
        out_shape=jax.ShapeDtypeStruct((batch, seq, D_MODEL), F32),
        scratch_shapes=[pltpu.VMEM((TS, EVEN_IN), F32),
                        pltpu.VMEM((TS, MIX_WIDTH), BF16),
                        pltpu.VMEM((A_TAIL + TS, A_KV_WIDTH), BF16),
                        pltpu.VMEM((A_TAIL + TS, A_KV_WIDTH), BF16),
                        pltpu.VMEM((A_TAIL + TS, 2 * A_KV_WIDTH), BF16),
                        pltpu.VMEM((A_TAIL + TS, 2 * A_KV_WIDTH), BF16),
                        pltpu.VMEM((B_HEADS, B_DIM, B_DIM), F32)],
        compiler_params=params,
        name="layer0",
    )(x, cos2, sin2, dmat, qdec, kdec, gdec, sink_cols,
      e_norm[0].reshape(1, D_MODEL), w_in0, e_w_out[0].astype(BF16),
      c_norm[0].reshape(1, D_MODEL), wq[0], wo[0], mkt, mv)

    out = pl.pallas_call(
        _layer1_kernel,
        grid=(batch, n_t),
        in_specs=[tile_spec,
                  _const_spec((1, D_MODEL)),
                  _const_spec((D_MODEL, 4 * D_MODEL)),
                  _const_spec((C_CONV, D_MODEL)),
                  _const_spec((1, D_MODEL)),
                  _const_spec((D_MODEL, D_MODEL)),
                  _const_spec((1, D_MODEL)),
                  _const_spec((D_MODEL, D_MODEL)),
                  _const_spec((D_MODEL, D_MODEL)),
                  mkt_spec(1), mv_spec(1),
                  _const_spec((1, D_MODEL))],
        out_specs=tile_spec,
        out_shape=jax.ShapeDtypeStruct((batch, seq, D_MODEL), F32),
        scratch_shapes=[pltpu.VMEM((TS, 4 * D_MODEL), F32),
                        pltpu.VMEM((C_PAD + TS, D_MODEL), F32)],
        compiler_params=params,
        name="layer1",
    )(x1, o_norm[0].reshape(1, D_MODEL), o_w_in[0].astype(BF16), o_conv_w[0],
      o_conv_b[0].reshape(1, D_MODEL), o_w_out[0].astype(BF16),
      c_norm[1].reshape(1, D_MODEL), wq[1], wo[1], mkt, mv, final_norm.reshape(1, D_MODEL))
    return x1, out


def kernel(x, mem, e_norm, e_w_in, e_sink, e_w_out, o_norm, o_w_in, o_conv_w, o_conv_b,
           o_w_out, c_norm, c_mem_norm, c_wq, c_wkv, c_wo, final_norm):
    return _forward(x, mem, e_norm, e_w_in, e_sink, e_w_out, o_norm, o_w_in, o_conv_w, o_conv_b,
                    o_w_out, c_norm, c_mem_norm, c_wq, c_wkv, c_wo, final_norm)[1]
```

```python
import functools

import jax
import jax.numpy as jnp
import numpy as np
from jax import lax
from jax.experimental import pallas as pl
from jax.experimental.pallas import tpu as pltpu

F32 = jnp.float32
BF16 = jnp.bfloat16

D_MODEL = 1024
CHUNK = 64
MEM_LEN = 256
EPS = 1e-6
NEG_INF = -1e30

A_Q_HEADS = 8
A_KV_HEADS = 2
A_HEAD_DIM = 64
A_WIN_CHUNKS = 2
A_WIDTH = A_Q_HEADS * A_HEAD_DIM
A_KV_WIDTH = A_KV_HEADS * A_HEAD_DIM
A_BAND = (A_WIN_CHUNKS + 1) * CHUNK
A_TAIL = A_WIN_CHUNKS * CHUNK

B_HEADS = 4
B_DIM = 128
B_WIDTH = B_HEADS * B_DIM
ROT_BASE = 10000.0
B_GROUP = 256

C_CONV = 3
C_PAD = 8

X_HEADS = 4
X_DIM = D_MODEL // X_HEADS

OFF_AQ = 0
OFF_AK = OFF_AQ + A_WIDTH
OFF_AV = OFF_AK + A_KV_WIDTH
OFF_AG = OFF_AV + A_KV_WIDTH
OFF_BQ = OFF_AG + A_WIDTH
OFF_BK = OFF_BQ + B_WIDTH
OFF_BV = OFF_BK + B_WIDTH
OFF_BG = OFF_BV + B_WIDTH
EVEN_IN = OFF_BG + B_WIDTH
MIX_WIDTH = A_WIDTH + B_WIDTH

TS = 512
SUB = 512
VMEM_LIMIT_BYTES = 56 * 1024 * 1024


def _rmsnorm(x, g):
    return x * lax.rsqrt(jnp.mean(x * x, axis=-1, keepdims=True) + EPS) * g


def _silu(x):
    return x * jax.nn.sigmoid(x)


def _dot(a, b):
    return jnp.dot(a, b, preferred_element_type=F32)


def _dot_nt(a, b):
    return lax.dot_general(a, b, (((1,), (1,)), ((), ())), preferred_element_type=F32)


def _memkv_kernel(mem_ref, g_ref, w_ref, kt_ref, v_ref):
    mn = _rmsnorm(mem_ref[0], g_ref[0]).astype(BF16)
    kv = _dot(mn, w_ref[0])
    kt_ref[0, 0] = kv[:, :D_MODEL].T.astype(BF16)
    v_ref[0, 0] = kv[:, D_MODEL:].astype(BF16)


def _cross_attention(x1, cnorm_ref, wq_ref, wo_ref, mkt_ref, mv_ref):
    xn = _rmsnorm(x1, cnorm_ref[...]).astype(BF16)
    q = _dot(xn, wq_ref[...]).astype(BF16)
    outs = []
    for h in range(X_HEADS):
        cs = slice(h * X_DIM, (h + 1) * X_DIM)
        s = _dot(q[:, cs], mkt_ref[0, cs, :])
        m = jnp.max(s, axis=-1, keepdims=True)
        p = jnp.exp(s - m)
        l = jnp.sum(p, axis=-1, keepdims=True)
        o = _dot(p.astype(BF16), mv_ref[0, :, cs]) / l
        outs.append(o.astype(BF16))
    o = jnp.concatenate(outs, axis=1)
    return x1 + _dot(o, wo_ref[...])


def _layer0_kernel(x_ref, cos_ref, sin_ref, dmat_ref, qdec_ref, kdec_ref, gdec_ref, sink_ref,
                   enorm_ref, win_ref, wout_ref, cnorm_ref, wq_ref, wo_ref, mkt_ref, mv_ref,
                   o_ref, h_ref, y_ref, k_ref, ksw_ref, v_ref, vsw_ref, state_ref):
    t = pl.program_id(1)

    @pl.when(t == 0)
    def _():
        zk = jnp.zeros((A_TAIL, A_KV_WIDTH), BF16)
        k_ref[0:A_TAIL, :] = zk
        ksw_ref[0:A_TAIL, :] = zk
        v_ref[0:A_TAIL, 0:A_KV_WIDTH] = zk
        vsw_ref[0:A_TAIL, 0:A_KV_WIDTH] = zk
        ones = jnp.ones((A_TAIL + TS, A_KV_WIDTH), BF16)
        v_ref[:, A_KV_WIDTH:] = ones
        vsw_ref[:, A_KV_WIDTH:] = ones
        state_ref[...] = jnp.zeros_like(state_ref)

    @pl.when(t > 0)
    def _():
        k_ref[0:A_TAIL, :] = k_ref[TS:TS + A_TAIL, :]
        ksw_ref[0:A_TAIL, :] = ksw_ref[TS:TS + A_TAIL, :]
        v_ref[0:A_TAIL, 0:A_KV_WIDTH] = v_ref[TS:TS + A_TAIL, 0:A_KV_WIDTH]
        vsw_ref[0:A_TAIL, 0:A_KV_WIDTH] = vsw_ref[TS:TS + A_TAIL, 0:A_KV_WIDTH]

    lane = lax.broadcasted_iota(jnp.int32, (CHUNK, 2 * A_HEAD_DIM), 1)
    lo = lane < A_HEAD_DIM
    kidx = lax.broadcasted_iota(jnp.int32, (4 * CHUNK, A_KV_WIDTH), 1)
    hi_half = kidx >= A_BAND - A_KV_WIDTH
    pad_k = jnp.zeros((2 * A_KV_WIDTH - A_BAND, A_KV_WIDTH), BF16)
    pad_r = lax.broadcasted_iota(jnp.int32, (2 * A_KV_WIDTH - A_BAND, 2 * A_KV_WIDTH), 0)
    pad_c = lax.broadcasted_iota(jnp.int32, (2 * A_KV_WIDTH - A_BAND, 2 * A_KV_WIDTH), 1)
    pad_v = jnp.where((pad_r == 0) & (pad_c >= A_KV_WIDTH), 1.0, 0.0).astype(BF16)

    def attend(lhs, kband, vband, sink_pad, n_masked):
        s = _dot_nt(lhs, jnp.concatenate([kband, pad_k], axis=0))
        s0 = s[:, :A_KV_WIDTH]
        s1 = jnp.where(hi_half, sink_pad, s[:, A_KV_WIDTH:])
        if n_masked is not None:
            s0 = jnp.where(kidx < n_masked, NEG_INF, s0)
        m = jnp.max(jnp.maximum(s0, s1), axis=-1, keepdims=True)
        p = jnp.concatenate([jnp.exp(s0 - m), jnp.exp(s1 - m)], axis=1).astype(BF16)
        pv = _dot(p, jnp.concatenate([vband, pad_v], axis=0))
        return pv[:, :A_KV_WIDTH] / pv[:, A_KV_WIDTH:]

    def swa_chunk(c):
        r0 = c * CHUNK
        q = h_ref[r0:r0 + CHUNK, OFF_AQ:OFF_AQ + A_WIDTH]
        qc = [q[:, j * 128:(j + 1) * 128] for j in range(4)]
        zero = jnp.zeros_like(qc[0])
        lhs_a = jnp.concatenate([jnp.where(lo, qc[0], zero), jnp.where(lo, qc[1], zero),
                                 jnp.where(lo, zero, qc[2]), jnp.where(lo, zero, qc[3])],
                                axis=0).astype(BF16)
        lhs_b = jnp.concatenate([jnp.where(lo, zero, qc[0]), jnp.where(lo, zero, qc[1]),
                                 jnp.where(lo, qc[2], zero), jnp.where(lo, qc[3], zero)],
                                axis=0).astype(BF16)
        n_masked = None
        if c < A_WIN_CHUNKS:
            n_masked = jnp.where(t == 0, (A_WIN_CHUNKS - c) * CHUNK, 0)
        band = slice(r0, r0 + A_BAND)
        oa = attend(lhs_a, k_ref[band, :], v_ref[band, :], sink_ref[0], n_masked)
        ob = attend(lhs_b, ksw_ref[band, :], vsw_ref[band, :], sink_ref[1], n_masked)
        ya = jnp.concatenate([
            jnp.where(lo, oa[0:64], ob[0:64]),
            jnp.where(lo, oa[64:128], ob[64:128]),
            jnp.where(lo, ob[128:192], oa[128:192]),
            jnp.where(lo, ob[192:256], oa[192:256])], axis=1)
        ag = h_ref[r0:r0 + CHUNK, OFF_AG:OFF_AG + A_WIDTH]
        y_ref[r0:r0 + CHUNK, 0:A_WIDTH] = (ya * _silu(ag)).astype(BF16)

    def retention_group(rows):
        cos = cos_ref[rows, :]
        sin = sin_ref[rows, :]
        for h in range(B_HEADS):
            cs = slice(h * B_DIM, (h + 1) * B_DIM)
            q = h_ref[rows, OFF_BQ + h * B_DIM:OFF_BQ + (h + 1) * B_DIM]
            k = h_ref[rows, OFF_BK + h * B_DIM:OFF_BK + (h + 1) * B_DIM]
            v = h_ref[rows, OFF_BV + h * B_DIM:OFF_BV + (h + 1) * B_DIM].astype(BF16)
            qr = q * cos + pltpu.roll(q, B_DIM // 2, 1) * sin
            kr = k * cos + pltpu.roll(k, B_DIM // 2, 1) * sin
            sc = _dot_nt(qr.astype(BF16), kr.astype(BF16)) * dmat_ref[h]
            state = state_ref[h]
            lhs = jnp.concatenate([sc.astype(BF16), (qr * qdec_ref[h]).astype(BF16)], axis=1)
            rhs = jnp.concatenate([v, state.astype(BF16)], axis=0)
            o = _dot(lhs, rhs)
            o = o * lax.rsqrt(jnp.mean(o * o, axis=-1, keepdims=True) + EPS)
            kd = (kr * kdec_ref[h]).T.astype(BF16)
            state_ref[h] = state * gdec_ref[h] + _dot(kd, v)
            bg = h_ref[rows, OFF_BG + h * B_DIM:OFF_BG + (h + 1) * B_DIM]
            y_ref[rows, A_WIDTH + h * B_DIM:A_WIDTH + (h + 1) * B_DIM] = (o * _silu(bg)).astype(BF16)

    for sub in range(TS // SUB):
        s0 = sub * SUB
        rows = slice(s0, s0 + SUB)
        x = x_ref[0, rows, :]
        xn = _rmsnorm(x, enorm_ref[...]).astype(BF16)
        h_ref[rows, :] = _dot(xn, win_ref[...])

        kk = h_ref[rows, OFF_AK:OFF_AK + A_KV_WIDTH]
        vv = h_ref[rows, OFF_AV:OFF_AV + A_KV_WIDTH]
        hist = slice(A_TAIL + s0, A_TAIL + s0 + SUB)
        k_ref[hist, :] = kk.astype(BF16)
        ksw_ref[hist, :] = pltpu.roll(kk, A_HEAD_DIM, 1).astype(BF16)
        v_ref[hist, 0:A_KV_WIDTH] = vv.astype(BF16)
        vsw_ref[hist, 0:A_KV_WIDTH] = pltpu.roll(vv, A_HEAD_DIM, 1).astype(BF16)

        for c in range(s0 // CHUNK, (s0 + SUB) // CHUNK):
            swa_chunk(c)
        for g0 in range(s0, s0 + SUB, B_GROUP):
            retention_group(slice(g0, g0 + B_GROUP))

        x1 = x + _dot(y_ref[rows, :], wout_ref[...])
        o_ref[0, rows, :] = _cross_attention(x1, cnorm_ref, wq_ref, wo_ref, mkt_ref, mv_ref)


def _layer1_kernel(x_ref, onorm_ref, win_ref, convw_ref, convb_ref, wout_ref,
                   cnorm_ref, wq_ref, wo_ref, mkt_ref, mv_ref, fnorm_ref,
                   o_ref, h_ref, z_ref):
    t = pl.program_id(1)

    @pl.when(t == 0)
    def _():
        z_ref[0:C_PAD, :] = jnp.zeros((C_PAD, D_MODEL), F32)

    @pl.when(t > 0)
    def _():
        z_ref[0:C_PAD, :] = z_ref[TS:TS + C_PAD, :]

    for sub in range(TS // SUB):
        s0 = sub * SUB
        rows = slice(s0, s0 + SUB)
        x = x_ref[0, rows, :]
        xn = _rmsnorm(x, onorm_ref[...]).astype(BF16)
        h_ref[rows, :] = _dot(xn, win_ref[...])
        z = h_ref[rows, D_MODEL:2 * D_MODEL] * h_ref[rows, 2 * D_MODEL:3 * D_MODEL]
        z_ref[C_PAD + s0:C_PAD + s0 + SUB, :] = z
        conv = convb_ref[...] + z * convw_ref[2:3, :]
        conv = conv + z_ref[C_PAD - 1 + s0:C_PAD - 1 + s0 + SUB, :] * convw_ref[1:2, :]
        conv = conv + z_ref[C_PAD - 2 + s0:C_PAD - 2 + s0 + SUB, :] * convw_ref[0:1, :]
        y = h_ref[rows, 0:D_MODEL] * conv * _silu(h_ref[rows, 3 * D_MODEL:4 * D_MODEL])
        x1 = x + _dot(y.astype(BF16), wout_ref[...])
        x2 = _cross_attention(x1, cnorm_ref, wq_ref, wo_ref, mkt_ref, mv_ref)
        o_ref[0, rows, :] = _rmsnorm(x2, fnorm_ref[...])


def _const_spec(shape):
    return pl.BlockSpec(shape, lambda *_: (0,) * len(shape), pipeline_mode=pl.Buffered(1))


def _retention_tables(seq):
    half = B_DIM // 2
    inv = 1.0 / (ROT_BASE ** jnp.linspace(0.0, 1.0, half, dtype=F32))
    ang = jnp.arange(seq).astype(F32)[:, None] * inv[None, :]
    cos, sin = jnp.cos(ang), jnp.sin(ang)
    cos2 = jnp.concatenate([cos, cos], axis=1)
    sin2 = jnp.concatenate([-sin, sin], axis=1)
    log_gamma = jnp.log1p(-jnp.exp2(-5.0 - jnp.arange(B_HEADS, dtype=F32)))
    idx = jnp.arange(B_GROUP, dtype=F32)
    chunk_of = jnp.arange(B_GROUP) // CHUNK
    visible = (chunk_of[None, :] <= chunk_of[:, None]).astype(F32)
    scale = B_DIM ** -0.5
    dmat = jnp.exp(log_gamma[:, None, None] * jnp.abs(idx[:, None] - idx[None, :])) * visible * scale
    qdec = jnp.exp(log_gamma[:, None] * (idx + 1.0)[None, :])
    kdec = jnp.exp(log_gamma[:, None] * (B_GROUP - 1 - idx)[None, :]) * scale
    qdec = jnp.broadcast_to(qdec[:, :, None], (B_HEADS, B_GROUP, B_DIM))
    kdec = jnp.broadcast_to(kdec[:, :, None], (B_HEADS, B_GROUP, B_DIM))
    gdec = jnp.broadcast_to(jnp.exp(log_gamma * B_GROUP)[:, None, None], (B_HEADS, 1, B_DIM))
    return cos2, sin2, dmat, qdec, kdec, gdec


def _forward(x, mem, e_norm, e_w_in, e_sink, e_w_out, o_norm, o_w_in, o_conv_w, o_conv_b,
             o_w_out, c_norm, c_mem_norm, c_wq, c_wkv, c_wo, final_norm):
    batch, seq, d = x.shape
    depth = c_wq.shape[0]
    assert d == D_MODEL and depth == 2 and seq % TS == 0 and mem.shape == (batch, MEM_LEN, D_MODEL)
    n_t = seq // TS
    params = pltpu.CompilerParams(dimension_semantics=("arbitrary", "arbitrary"),
                                  vmem_limit_bytes=VMEM_LIMIT_BYTES)

    mkt, mv = pl.pallas_call(
        _memkv_kernel,
        grid=(depth, batch),
        in_specs=[pl.BlockSpec((1, MEM_LEN, D_MODEL), lambda i, b: (b, 0, 0)),
                  pl.BlockSpec((1, 1, D_MODEL), lambda i, b: (i, 0, 0)),
                  pl.BlockSpec((1, D_MODEL, 2 * D_MODEL), lambda i, b: (i, 0, 0))],
        out_specs=[pl.BlockSpec((1, 1, D_MODEL, MEM_LEN), lambda i, b: (i, b, 0, 0)),
                   pl.BlockSpec((1, 1, MEM_LEN, D_MODEL), lambda i, b: (i, b, 0, 0))],
        out_shape=[jax.ShapeDtypeStruct((depth, batch, D_MODEL, MEM_LEN), BF16),
                   jax.ShapeDtypeStruct((depth, batch, MEM_LEN, D_MODEL), BF16)],
        compiler_params=params,
        name="memkv",
    )(mem, c_mem_norm.reshape(depth, 1, D_MODEL), c_wkv.astype(BF16))

    wq = (c_wq * (X_DIM ** -0.5)).astype(BF16)
    wo = c_wo.astype(BF16)

    col_scale = jnp.concatenate([jnp.full((A_WIDTH,), A_HEAD_DIM ** -0.5, F32),
                                 jnp.ones((EVEN_IN - A_WIDTH,), F32)])
    w_in0 = (e_w_in[0] * col_scale[None, :]).astype(BF16)
    cos2, sin2, dmat, qdec, kdec, gdec = _retention_tables(seq)
    sink = e_sink[0].astype(F32)
    sink_rows = jnp.stack([jnp.repeat(sink[jnp.array([0, 2, 5, 7])], CHUNK),
                           jnp.repeat(sink[jnp.array([1, 3, 4, 6])], CHUNK)])
    sink_pad = jnp.where(jnp.arange(A_KV_WIDTH)[None, None, :] == A_BAND - A_KV_WIDTH,
                         sink_rows[:, :, None], NEG_INF).astype(F32)

    tile_spec = pl.BlockSpec((1, TS, D_MODEL), lambda b, t: (b, t, 0))
    mkt_spec = lambda i: pl.BlockSpec((None, 1, D_MODEL, MEM_LEN), lambda b, t: (i, b, 0, 0))
    mv_spec = lambda i: pl.BlockSpec((None, 1, MEM_LEN, D_MODEL), lambda b, t: (i, b, 0, 0))

    x1 = pl.pallas_call(
        _layer0_kernel,
        grid=(batch, n_t),
        in_specs=[tile_spec,
                  pl.BlockSpec((TS, B_DIM), lambda b, t: (t, 0)),
                  pl.BlockSpec((TS, B_DIM), lambda b, t: (t, 0)),
                  _const_spec((B_HEADS, B_GROUP, B_GROUP)),
                  _const_spec((B_HEADS, B_GROUP, B_DIM)),
                  _const_spec((B_HEADS, B_GROUP, B_DIM)),
                  _const_spec((B_HEADS, 1, B_DIM)),
                  _const_spec((2, 4 * CHUNK, A_KV_WIDTH)),
                  _const_spec((1, D_MODEL)),
                  _const_spec((D_MODEL, EVEN_IN)),
                  _const_spec((MIX_WIDTH, D_MODEL)),
                  _const_spec((1, D_MODEL)),
                  _const_spec((D_MODEL, D_MODEL)),
                  _const_spec((D_MODEL, D_MODEL)),
                  mkt_spec(0), mv_spec(0)],
        out_specs=tile_spec,
        out_shape=jax.ShapeDtypeStruct((batch, seq, D_MODEL), F32),
        scratch_shapes=[pltpu.VMEM((TS, EVEN_IN), F32),
                        pltpu.VMEM((TS, MIX_WIDTH), BF16),
                        pltpu.VMEM((A_TAIL + TS, A_KV_WIDTH), BF16),
                        pltpu.VMEM((A_TAIL + TS, A_KV_WIDTH), BF16),
                        pltpu.VMEM((A_TAIL + TS, 2 * A_KV_WIDTH), BF16),
                        pltpu.VMEM((A_TAIL + TS, 2 * A_KV_WIDTH), BF16),
                        pltpu.VMEM((B_HEADS, B_DIM, B_DIM), F32)],
        compiler_params=params,
        name="layer0",
    )(x, cos2, sin2, dmat, qdec, kdec, gdec, sink_pad,
      e_norm[0].reshape(1, D_MODEL), w_in0, e_w_out[0].astype(BF16),
      c_norm[0].reshape(1, D_MODEL), wq[0], wo[0], mkt, mv)

    out = pl.pallas_call(
        _layer1_kernel,
        grid=(batch, n_t),
        in_specs=[tile_spec,
                  _const_spec((1, D_MODEL)),
                  _const_spec((D_MODEL, 4 * D_MODEL)),
                  _const_spec((C_CONV, D_MODEL)),
                  _const_spec((1, D_MODEL)),
                  _const_spec((D_MODEL, D_MODEL)),
                  _const_spec((1, D_MODEL)),
                  _const_spec((D_MODEL, D_MODEL)),
                  _const_spec((D_MODEL, D_MODEL)),
                  mkt_spec(1), mv_spec(1),
                  _const_spec((1, D_MODEL))],
        out_specs=tile_spec,
        out_shape=jax.ShapeDtypeStruct((batch, seq, D_MODEL), F32),
        scratch_shapes=[pltpu.VMEM((TS, 4 * D_MODEL), F32),
                        pltpu.VMEM((C_PAD + TS, D_MODEL), F32)],
        compiler_params=params,
        name="layer1",
    )(x1, o_norm[0].reshape(1, D_MODEL), o_w_in[0].astype(BF16), o_conv_w[0],
      o_conv_b[0].reshape(1, D_MODEL), o_w_out[0].astype(BF16),
      c_norm[1].reshape(1, D_MODEL), wq[1], wo[1], mkt, mv, final_norm.reshape(1, D_MODEL))
    return x1, out


def kernel(x, mem, e_norm, e_w_in, e_sink, e_w_out, o_norm, o_w_in, o_conv_w, o_conv_b,
           o_w_out, c_norm, c_mem_norm, c_wq, c_wkv, c_wo, final_norm):
    return _forward(x, mem, e_norm, e_w_in, e_sink, e_w_out, o_norm, o_w_in, o_conv_w, o_conv_b,
                    o_w_out, c_norm, c_mem_norm, c_wq, c_wkv, c_wo, final_norm)[1]
```

```python
import functools

import jax
import jax.numpy as jnp
import numpy as np
from jax import lax
from jax.experimental import pallas as pl
from jax.experimental.pallas import tpu as pltpu

F32 = jnp.float32
BF16 = jnp.bfloat16

D_MODEL = 1024
CHUNK = 64
MEM_LEN = 256
EPS = 1e-6
NEG_INF = -1e30

A_Q_HEADS = 8
A_KV_HEADS = 2
A_HEAD_DIM = 64
A_WIN_CHUNKS = 2
A_WIDTH = A_Q_HEADS * A_HEAD_DIM
A_KV_WIDTH = A_KV_HEADS * A_HEAD_DIM
A_BAND = (A_WIN_CHUNKS + 1) * CHUNK
A_TAIL = A_WIN_CHUNKS * CHUNK

B_HEADS = 4
B_DIM = 128
B_WIDTH = B_HEADS * B_DIM
ROT_BASE = 10000.0
B_GROUP = 256

C_CONV = 3
C_PAD = 8

X_HEADS = 4
X_DIM = D_MODEL // X_HEADS

OFF_AQ = 0
OFF_AK = OFF_AQ + A_WIDTH
OFF_AV = OFF_AK + A_KV_WIDTH
OFF_AG = OFF_AV + A_KV_WIDTH
OFF_BQ = OFF_AG + A_WIDTH
OFF_BK = OFF_BQ + B_WIDTH
OFF_BV = OFF_BK + B_WIDTH
OFF_BG = OFF_BV + B_WIDTH
EVEN_IN = OFF_BG + B_WIDTH
MIX_WIDTH = A_WIDTH + B_WIDTH

TS = 512
SUB0 = 256
SUB1 = 256
VMEM_LIMIT_BYTES = 56 * 1024 * 1024


def _rmsnorm(x, g):
    return x * lax.rsqrt(jnp.mean(x * x, axis=-1, keepdims=True) + EPS) * g


def _silu(x):
    return x * jax.nn.sigmoid(x)


def _dot(a, b):
    return jnp.dot(a, b, preferred_element_type=F32)


def _dot_nt(a, b):
    return lax.dot_general(a, b, (((1,), (1,)), ((), ())), preferred_element_type=F32)


def _memkv_kernel(mem_ref, g_ref, w_ref, kt_ref, v_ref):
    mn = _rmsnorm(mem_ref[0], g_ref[0]).astype(BF16)
    kv = _dot(mn, w_ref[0])
    kt_ref[0, 0] = kv[:, :D_MODEL].T.astype(BF16)
    v_ref[0, 0] = kv[:, D_MODEL:].astype(BF16)


def _cross_attention(x1, cnorm_ref, wq_ref, wo_ref, mkt_ref, mv_ref):
    xn = _rmsnorm(x1, cnorm_ref[...]).astype(BF16)
    q = _dot(xn, wq_ref[...]).astype(BF16)
    outs = []
    for h in range(X_HEADS):
        cs = slice(h * X_DIM, (h + 1) * X_DIM)
        s = _dot(q[:, cs], mkt_ref[0, cs, :])
        m = jnp.max(s, axis=-1, keepdims=True)
        p = jnp.exp(s - m)
        l = jnp.sum(p, axis=-1, keepdims=True)
        o = _dot(p.astype(BF16), mv_ref[0, :, cs]) / l
        outs.append(o.astype(BF16))
    o = jnp.concatenate(outs, axis=1)
    return x1 + _dot(o, wo_ref[...])


def _cross_attention_blocks(x1s, cnorm_ref, wq_ref, wo_ref, mkt_ref, mv_ref):
    qs = [_dot(_rmsnorm(x1, cnorm_ref[...]).astype(BF16), wq_ref[...]).astype(BF16) for x1 in x1s]
    os = []
    for q in qs:
        outs = []
        for h in range(X_HEADS):
            cs = slice(h * X_DIM, (h + 1) * X_DIM)
            s = _dot(q[:, cs], mkt_ref[0, cs, :])
            m = jnp.max(s, axis=-1, keepdims=True)
            p = jnp.exp(s - m)
            l = jnp.sum(p, axis=-1, keepdims=True)
            outs.append((_dot(p.astype(BF16), mv_ref[0, :, cs]) / l).astype(BF16))
        os.append(jnp.concatenate(outs, axis=1))
    return [x1 + _dot(o, wo_ref[...]) for x1, o in zip(x1s, os)]


def _layer0_kernel(x_ref, cos_ref, sin_ref, dmat_ref, qdec_ref, kdec_ref, gdec_ref, sink_ref,
                   enorm_ref, win_ref, wout_ref, cnorm_ref, wq_ref, wo_ref, mkt_ref, mv_ref,
                   o_ref, h_ref, y_ref, k_ref, ksw_ref, v_ref, vsw_ref, state_ref):
    t = pl.program_id(1)

    @pl.when(t == 0)
    def _():
        zk = jnp.zeros((A_TAIL, A_KV_WIDTH), BF16)
        k_ref[0:A_TAIL, :] = zk
        ksw_ref[0:A_TAIL, :] = zk
        v_ref[0:A_TAIL, 0:A_KV_WIDTH] = zk
        vsw_ref[0:A_TAIL, 0:A_KV_WIDTH] = zk
        ones = jnp.ones((A_TAIL + TS, A_KV_WIDTH), BF16)
        v_ref[:, A_KV_WIDTH:] = ones
        vsw_ref[:, A_KV_WIDTH:] = ones
        state_ref[...] = jnp.zeros_like(state_ref)

    @pl.when(t > 0)
    def _():
        k_ref[0:A_TAIL, :] = k_ref[TS:TS + A_TAIL, :]
        ksw_ref[0:A_TAIL, :] = ksw_ref[TS:TS + A_TAIL, :]
        v_ref[0:A_TAIL, 0:A_KV_WIDTH] = v_ref[TS:TS + A_TAIL, 0:A_KV_WIDTH]
        vsw_ref[0:A_TAIL, 0:A_KV_WIDTH] = vsw_ref[TS:TS + A_TAIL, 0:A_KV_WIDTH]

    lane = lax.broadcasted_iota(jnp.int32, (CHUNK, 2 * A_HEAD_DIM), 1)
    lo = lane < A_HEAD_DIM
    kidx = lax.broadcasted_iota(jnp.int32, (4 * CHUNK, A_KV_WIDTH), 1)
    hi_half = kidx >= A_BAND - A_KV_WIDTH
    pad_k = jnp.zeros((2 * A_KV_WIDTH - A_BAND, A_KV_WIDTH), BF16)
    pad_r = lax.broadcasted_iota(jnp.int32, (2 * A_KV_WIDTH - A_BAND, 2 * A_KV_WIDTH), 0)
    pad_c = lax.broadcasted_iota(jnp.int32, (2 * A_KV_WIDTH - A_BAND, 2 * A_KV_WIDTH), 1)
    pad_v = jnp.where((pad_r == 0) & (pad_c >= A_KV_WIDTH), 1.0, 0.0).astype(BF16)

    def attend(lhs, kband, vband, sink_pad, n_masked):
        s = _dot_nt(lhs, jnp.concatenate([kband, pad_k], axis=0))
        s0 = s[:, :A_KV_WIDTH]
        s1 = jnp.where(hi_half, sink_pad, s[:, A_KV_WIDTH:])
        if n_masked is not None:
            s0 = jnp.where(kidx < n_masked, NEG_INF, s0)
        m = jnp.max(jnp.maximum(s0, s1), axis=-1, keepdims=True)
        p = jnp.concatenate([jnp.exp(s0 - m), jnp.exp(s1 - m)], axis=1).astype(BF16)
        pv = _dot(p, jnp.concatenate([vband, pad_v], axis=0))
        return pv[:, :A_KV_WIDTH] / pv[:, A_KV_WIDTH:]

    def swa_chunk(c):
        r0 = c * CHUNK
        q = h_ref[r0:r0 + CHUNK, OFF_AQ:OFF_AQ + A_WIDTH]
        qc = [q[:, j * 128:(j + 1) * 128] for j in range(4)]
        zero = jnp.zeros_like(qc[0])
        lhs_a = jnp.concatenate([jnp.where(lo, qc[0], zero), jnp.where(lo, qc[1], zero),
                                 jnp.where(lo, zero, qc[2]), jnp.where(lo, zero, qc[3])],
                                axis=0).astype(BF16)
        lhs_b = jnp.concatenate([jnp.where(lo, zero, qc[0]), jnp.where(lo, zero, qc[1]),
                                 jnp.where(lo, qc[2], zero), jnp.where(lo, qc[3], zero)],
                                axis=0).astype(BF16)
        n_masked = None
        if c < A_WIN_CHUNKS:
            n_masked = jnp.where(t == 0, (A_WIN_CHUNKS - c) * CHUNK, 0)
        band = slice(r0, r0 + A_BAND)
        oa = attend(lhs_a, k_ref[band, :], v_ref[band, :], sink_ref[0], n_masked)
        ob = attend(lhs_b, ksw_ref[band, :], vsw_ref[band, :], sink_ref[1], n_masked)
        ya = jnp.concatenate([
            jnp.where(lo, oa[0:64], ob[0:64]),
            jnp.where(lo, oa[64:128], ob[64:128]),
            jnp.where(lo, ob[128:192], oa[128:192]),
            jnp.where(lo, ob[192:256], oa[192:256])], axis=1)
        ag = h_ref[r0:r0 + CHUNK, OFF_AG:OFF_AG + A_WIDTH]
        y_ref[r0:r0 + CHUNK, 0:A_WIDTH] = (ya * _silu(ag)).astype(BF16)

    def retention_group(rows):
        cos = cos_ref[rows, :]
        sin = sin_ref[rows, :]
        for h in range(B_HEADS):
            cs = slice(h * B_DIM, (h + 1) * B_DIM)
            q = h_ref[rows, OFF_BQ + h * B_DIM:OFF_BQ + (h + 1) * B_DIM]
            k = h_ref[rows, OFF_BK + h * B_DIM:OFF_BK + (h + 1) * B_DIM]
            v = h_ref[rows, OFF_BV + h * B_DIM:OFF_BV + (h + 1) * B_DIM].astype(BF16)
            qr = q * cos + pltpu.roll(q, B_DIM // 2, 1) * sin
            kr = k * cos + pltpu.roll(k, B_DIM // 2, 1) * sin
            sc = _dot_nt(qr.astype(BF16), kr.astype(BF16)) * dmat_ref[h]
            state = state_ref[h]
            lhs = jnp.concatenate([sc.astype(BF16), (qr * qdec_ref[h]).astype(BF16)], axis=1)
            rhs = jnp.concatenate([v, state.astype(BF16)], axis=0)
            o = _dot(lhs, rhs)
            o = o * lax.rsqrt(jnp.mean(o * o, axis=-1, keepdims=True) + EPS)
            kd = (kr * kdec_ref[h]).T.astype(BF16)
            state_ref[h] = state * gdec_ref[h] + _dot(kd, v)
            bg = h_ref[rows, OFF_BG + h * B_DIM:OFF_BG + (h + 1) * B_DIM]
            y_ref[rows, A_WIDTH + h * B_DIM:A_WIDTH + (h + 1) * B_DIM] = (o * _silu(bg)).astype(BF16)

    SUB = SUB0
    blocks = [slice(s0, s0 + SUB) for s0 in range(0, TS, SUB)]
    xs = [x_ref[0, rows, :] for rows in blocks]
    for rows, x in zip(blocks, xs):
        s0 = rows.start
        xn = _rmsnorm(x, enorm_ref[...]).astype(BF16)
        h_ref[rows, :] = _dot(xn, win_ref[...])
        kk = h_ref[rows, OFF_AK:OFF_AK + A_KV_WIDTH]
        vv = h_ref[rows, OFF_AV:OFF_AV + A_KV_WIDTH]
        hist = slice(A_TAIL + s0, A_TAIL + s0 + SUB)
        k_ref[hist, :] = kk.astype(BF16)
        ksw_ref[hist, :] = pltpu.roll(kk, A_HEAD_DIM, 1).astype(BF16)
        v_ref[hist, 0:A_KV_WIDTH] = vv.astype(BF16)
        vsw_ref[hist, 0:A_KV_WIDTH] = pltpu.roll(vv, A_HEAD_DIM, 1).astype(BF16)
    x1s = []
    for rows, x in zip(blocks, xs):
        s0 = rows.start
        for c in range(s0 // CHUNK, (s0 + SUB) // CHUNK):
            swa_chunk(c)
        for g0 in range(s0, s0 + SUB, B_GROUP):
            retention_group(slice(g0, g0 + B_GROUP))
        x1s.append(x + _dot(y_ref[rows, :], wout_ref[...]))
    x2s = _cross_attention_blocks(x1s, cnorm_ref, wq_ref, wo_ref, mkt_ref, mv_ref)
    for rows, x2 in zip(blocks, x2s):
        o_ref[0, rows, :] = x2


def _layer1_kernel(x_ref, onorm_ref, win_ref, convw_ref, convb_ref, wout_ref,
                   cnorm_ref, wq_ref, wo_ref, mkt_ref, mv_ref, fnorm_ref,
                   o_ref, h_ref, z_ref):
    t = pl.program_id(1)

    @pl.when(t == 0)
    def _():
        z_ref[0:C_PAD, :] = jnp.zeros((C_PAD, D_MODEL), F32)

    @pl.when(t > 0)
    def _():
        z_ref[0:C_PAD, :] = z_ref[TS:TS + C_PAD, :]

    SUB = SUB1
    blocks = [slice(s0, s0 + SUB) for s0 in range(0, TS, SUB)]
    xs = [x_ref[0, rows, :] for rows in blocks]
    for rows, x in zip(blocks, xs):
        xn = _rmsnorm(x, onorm_ref[...]).astype(BF16)
        h_ref[rows, :] = _dot(xn, win_ref[...])
    x1s = []
    for rows, x in zip(blocks, xs):
        s0 = rows.start
        z = h_ref[rows, D_MODEL:2 * D_MODEL] * h_ref[rows, 2 * D_MODEL:3 * D_MODEL]
        z_ref[C_PAD + s0:C_PAD + s0 + SUB, :] = z
        conv = convb_ref[...] + z * convw_ref[2:3, :]
        conv = conv + z_ref[C_PAD - 1 + s0:C_PAD - 1 + s0 + SUB, :] * convw_ref[1:2, :]
        conv = conv + z_ref[C_PAD - 2 + s0:C_PAD - 2 + s0 + SUB, :] * convw_ref[0:1, :]
        y = h_ref[rows, 0:D_MODEL] * conv * _silu(h_ref[rows, 3 * D_MODEL:4 * D_MODEL])
        x1s.append(x + _dot(y.astype(BF16), wout_ref[...]))
    x2s = _cross_attention_blocks(x1s, cnorm_ref, wq_ref, wo_ref, mkt_ref, mv_ref)
    for rows, x2 in zip(blocks, x2s):
        o_ref[0, rows, :] = _rmsnorm(x2, fnorm_ref[...])


def _const_spec(shape):
    return pl.BlockSpec(shape, lambda *_: (0,) * len(shape), pipeline_mode=pl.Buffered(1))


def _retention_tables(seq):
    half = B_DIM // 2
    inv = 1.0 / (ROT_BASE ** jnp.linspace(0.0, 1.0, half, dtype=F32))
    ang = jnp.arange(seq).astype(F32)[:, None] * inv[None, :]
    cos, sin = jnp.cos(ang), jnp.sin(ang)
    cos2 = jnp.concatenate([cos, cos], axis=1)
    sin2 = jnp.concatenate([-sin, sin], axis=1)
    log_gamma = jnp.log1p(-jnp.exp2(-5.0 - jnp.arange(B_HEADS, dtype=F32)))
    idx = jnp.arange(B_GROUP, dtype=F32)
    chunk_of = jnp.arange(B_GROUP) // CHUNK
    visible = (chunk_of[None, :] <= chunk_of[:, None]).astype(F32)
    scale = B_DIM ** -0.5
    dmat = jnp.exp(log_gamma[:, None, None] * jnp.abs(idx[:, None] - idx[None, :])) * visible * scale
    qdec = jnp.exp(log_gamma[:, None] * (idx + 1.0)[None, :])
    kdec = jnp.exp(log_gamma[:, None] * (B_GROUP - 1 - idx)[None, :]) * scale
    qdec = jnp.broadcast_to(qdec[:, :, None], (B_HEADS, B_GROUP, B_DIM))
    kdec = jnp.broadcast_to(kdec[:, :, None], (B_HEADS, B_GROUP, B_DIM))
    gdec = jnp.broadcast_to(jnp.exp(log_gamma * B_GROUP)[:, None, None], (B_HEADS, 1, B_DIM))
    return cos2, sin2, dmat, qdec, kdec, gdec


def _forward(x, mem, e_norm, e_w_in, e_sink, e_w_out, o_norm, o_w_in, o_conv_w, o_conv_b,
             o_w_out, c_norm, c_mem_norm, c_wq, c_wkv, c_wo, final_norm):
    batch, seq, d = x.shape
    depth = c_wq.shape[0]
    assert d == D_MODEL and depth == 2 and seq % TS == 0 and mem.shape == (batch, MEM_LEN, D_MODEL)
    n_t = seq // TS
    params = pltpu.CompilerParams(dimension_semantics=("arbitrary", "arbitrary"),
                                  vmem_limit_bytes=VMEM_LIMIT_BYTES)

    mkt, mv = pl.pallas_call(
        _memkv_kernel,
        grid=(depth, batch),
        in_specs=[pl.BlockSpec((1, MEM_LEN, D_MODEL), lambda i, b: (b, 0, 0)),
                  pl.BlockSpec((1, 1, D_MODEL), lambda i, b: (i, 0, 0)),
                  pl.BlockSpec((1, D_MODEL, 2 * D_MODEL), lambda i, b: (i, 0, 0))],
        out_specs=[pl.BlockSpec((1, 1, D_MODEL, MEM_LEN), lambda i, b: (i, b, 0, 0)),
                   pl.BlockSpec((1, 1, MEM_LEN, D_MODEL), lambda i, b: (i, b, 0, 0))],
        out_shape=[jax.ShapeDtypeStruct((depth, batch, D_MODEL, MEM_LEN), BF16),
                   jax.ShapeDtypeStruct((depth, batch, MEM_LEN, D_MODEL), BF16)],
        compiler_params=params,
        name="memkv",
    )(mem, c_mem_norm.reshape(depth, 1, D_MODEL), c_wkv.astype(BF16))

    wq = (c_wq * (X_DIM ** -0.5)).astype(BF16)
    wo = c_wo.astype(BF16)

    col_scale = jnp.concatenate([jnp.full((A_WIDTH,), A_HEAD_DIM ** -0.5, F32),
                                 jnp.ones((EVEN_IN - A_WIDTH,), F32)])
    w_in0 = (e_w_in[0] * col_scale[None, :]).astype(BF16)
    cos2, sin2, dmat, qdec, kdec, gdec = _retention_tables(seq)
    sink = e_sink[0].astype(F32)
    sink_rows = jnp.stack([jnp.repeat(sink[jnp.array([0, 2, 5, 7])], CHUNK),
                           jnp.repeat(sink[jnp.array([1, 3, 4, 6])], CHUNK)])
    sink_pad = jnp.where(jnp.arange(A_KV_WIDTH)[None, None, :] == A_BAND - A_KV_WIDTH,
                         sink_rows[:, :, None], NEG_INF).astype(F32)

    tile_spec = pl.BlockSpec((1, TS, D_MODEL), lambda b, t: (b, t, 0))
    mkt_spec = lambda i: pl.BlockSpec((None, 1, D_MODEL, MEM_LEN), lambda b, t: (i, b, 0, 0))
    mv_spec = lambda i: pl.BlockSpec((None, 1, MEM_LEN, D_MODEL), lambda b, t: (i, b, 0, 0))

    x1 = pl.pallas_call(
        _layer0_kernel,
        grid=(batch, n_t),
        in_specs=[tile_spec,
                  pl.BlockSpec((TS, B_DIM), lambda b, t: (t, 0)),
                  pl.BlockSpec((TS, B_DIM), lambda b, t: (t, 0)),
                  _const_spec((B_HEADS, B_GROUP, B_GROUP)),
                  _const_spec((B_HEADS, B_GROUP, B_DIM)),
                  _const_spec((B_HEADS, B_GROUP, B_DIM)),
                  _const_spec((B_HEADS, 1, B_DIM)),
                  _const_spec((2, 4 * CHUNK, A_KV_WIDTH)),
                  _const_spec((1, D_MODEL)),
                  _const_spec((D_MODEL, EVEN_IN)),
                  _const_spec((MIX_WIDTH, D_MODEL)),
                  _const_spec((1, D_MODEL)),
                  _const_spec((D_MODEL, D_MODEL)),
                  _const_spec((D_MODEL, D_MODEL)),
                  mkt_spec(0), mv_spec(0)],
        out_specs=tile_spec,
        out_shape=jax.ShapeDtypeStruct((batch, seq, D_MODEL), F32),
        scratch_shapes=[pltpu.VMEM((TS, EVEN_IN), F32),
                        pltpu.VMEM((TS, MIX_WIDTH), BF16),
                        pltpu.VMEM((A_TAIL + TS, A_KV_WIDTH), BF16),
                        pltpu.VMEM((A_TAIL + TS, A_KV_WIDTH), BF16),
                        pltpu.VMEM((A_TAIL + TS, 2 * A_KV_WIDTH), BF16),
                        pltpu.VMEM((A_TAIL + TS, 2 * A_KV_WIDTH), BF16),
                        pltpu.VMEM((B_HEADS, B_DIM, B_DIM), F32)],
        compiler_params=params,
        name="layer0",
    )(x, cos2, sin2, dmat, qdec, kdec, gdec, sink_pad,
      e_norm[0].reshape(1, D_MODEL), w_in0, e_w_out[0].astype(BF16),
      c_norm[0].reshape(1, D_MODEL), wq[0], wo[0], mkt, mv)

    out = pl.pallas_call(
        _layer1_kernel,
        grid=(batch, n_t),
        in_specs=[tile_spec,
                  _const_spec((1, D_MODEL)),
                  _const_spec((D_MODEL, 4 * D_MODEL)),
                  _const_spec((C_CONV, D_MODEL)),
                  _const_spec((1, D_MODEL)),
                  _const_spec((D_MODEL, D_MODEL)),
                  _const_spec((1, D_MODEL)),
                  _const_spec((D_MODEL, D_MODEL)),
                  _const_spec((D_MODEL, D_MODEL)),
                  mkt_spec(1), mv_spec(1),
                  _const_spec((1, D_MODEL))],
        out_specs=tile_spec,
        out_shape=jax.ShapeDtypeStruct((batch, seq, D_MODEL), F32),
        scratch_shapes=[pltpu.VMEM((TS, 4 * D_MODEL), F32),
                        pltpu.VMEM((C_PAD + TS, D_MODEL), F32)],
        compiler_params=params,
        name="layer1",
    )(x1, o_norm[0].reshape(1, D_MODEL), o_w_in[0].astype(BF16), o_conv_w[0],
      o_conv_b[0].reshape(1, D_MODEL), o_w_out[0].astype(BF16),
      c_norm[1].reshape(1, D_MODEL), wq[1], wo[1], mkt, mv, final_norm.reshape(1, D_MODEL))
    return x1, out


def kernel(x, mem, e_norm, e_w_in, e_sink, e_w_out, o_norm, o_w_in, o_conv_w, o_conv_b,
           o_w_out, c_norm, c_mem_norm, c_wq, c_wkv, c_wo, final_norm):
    return _forward(x, mem, e_norm, e_w_in, e_sink, e_w_out, o_norm, o_w_in, o_conv_w, o_conv_b,
                    o_w_out, c_norm, c_mem_norm, c_wq, c_wkv, c_wo, final_norm)[1]
```

```python
import jax
import jax.numpy as jnp
import numpy as np
from jax import lax
from jax.experimental import pallas as pl
from jax.experimental.pallas import tpu as pltpu

F32 = jnp.float32
BF16 = jnp.bfloat16

D_MODEL = 1024
CHUNK = 64
MEM_LEN = 256
EPS = 1e-6
NEG_INF = -1e30
LOG2E = 1.4426950408889634

A_Q_HEADS = 8
A_KV_HEADS = 2
A_HEAD_DIM = 64
A_WIN_CHUNKS = 2
A_WIDTH = A_Q_HEADS * A_HEAD_DIM
A_KV_WIDTH = A_KV_HEADS * A_HEAD_DIM
A_BAND = (A_WIN_CHUNKS + 1) * CHUNK
A_TAIL = A_WIN_CHUNKS * CHUNK

B_HEADS = 4
B_DIM = 128
B_WIDTH = B_HEADS * B_DIM
ROT_BASE = 10000.0
B_GROUP = 256

C_CONV = 3
C_PAD = 8

X_HEADS = 4
X_DIM = D_MODEL // X_HEADS

OFF_AQ = 0
OFF_AK = OFF_AQ + A_WIDTH
OFF_AV = OFF_AK + A_KV_WIDTH
OFF_AG = OFF_AV + A_KV_WIDTH
OFF_BQ = OFF_AG + A_WIDTH
OFF_BK = OFF_BQ + B_WIDTH
OFF_BV = OFF_BK + B_WIDTH
OFF_BG = OFF_BV + B_WIDTH
EVEN_IN = OFF_BG + B_WIDTH
MIX_WIDTH = A_WIDTH + B_WIDTH

TS = 512
SUB0 = 256
SUB1 = 256
VMEM_LIMIT_BYTES = 56 * 1024 * 1024


def _rmsnorm(x, g):
    return x * lax.rsqrt(jnp.mean(x * x, axis=-1, keepdims=True) + EPS) * g


def _silu(x):
    return x * jax.nn.sigmoid(x)


def _dot(a, b):
    return jnp.dot(a, b, preferred_element_type=F32)


def _dot_nt(a, b):
    return lax.dot_general(a, b, (((1,), (1,)), ((), ())), preferred_element_type=F32)


def _memkv_kernel(mem_ref, g_ref, w_ref, kt_ref, v_ref):
    mn = _rmsnorm(mem_ref[0], g_ref[0]).astype(BF16)
    kv = _dot(mn, w_ref[0])
    kt_ref[0, 0] = kv[:, :D_MODEL].T.astype(BF16)
    v_ref[0, 0] = kv[:, D_MODEL:].astype(BF16)


def _cross_attention_blocks(x1s, cnorm_ref, wq_ref, wo_ref, mkt_ref, mv_ref):
    qs = [_dot(_rmsnorm(x1, cnorm_ref[...]).astype(BF16), wq_ref[...]).astype(BF16) for x1 in x1s]
    os = []
    for q in qs:
        outs = []
        for h in range(X_HEADS):
            cs = slice(h * X_DIM, (h + 1) * X_DIM)
            s = _dot(q[:, cs], mkt_ref[0, cs, :])
            m = jnp.max(s, axis=-1, keepdims=True)
            p = jnp.exp2(s - m)
            l = jnp.sum(p, axis=-1, keepdims=True)
            outs.append((_dot(p.astype(BF16), mv_ref[0, :, cs]) / l).astype(BF16))
        os.append(jnp.concatenate(outs, axis=1))
    return [x1 + _dot(o, wo_ref[...]) for x1, o in zip(x1s, os)]


def _layer0_kernel(x_ref, cos_ref, sin_ref, dmat_ref, qdec_ref, kdec_ref, gdec_ref, sink_ref,
                   enorm_ref, win_ref, wout_ref, cnorm_ref, wq_ref, wo_ref, mkt_ref, mv_ref,
                   o_ref, h_ref, y_ref, k_ref, v_ref, state_ref):
    t = pl.program_id(1)

    @pl.when(t == 0)
    def _():
        zk = jnp.zeros((A_TAIL, A_KV_WIDTH), BF16)
        k_ref[0:A_TAIL, :] = zk
        v_ref[0:A_TAIL, 0:A_KV_WIDTH] = zk
        v_ref[:, A_KV_WIDTH:] = jnp.ones((A_TAIL + TS, A_KV_WIDTH), BF16)
        state_ref[...] = jnp.zeros_like(state_ref)

    @pl.when(t > 0)
    def _():
        k_ref[0:A_TAIL, :] = k_ref[TS:TS + A_TAIL, :]
        v_ref[0:A_TAIL, 0:A_KV_WIDTH] = v_ref[TS:TS + A_TAIL, 0:A_KV_WIDTH]

    lane = lax.broadcasted_iota(jnp.int32, (CHUNK, A_KV_WIDTH), 1)
    lo = lane < A_HEAD_DIM
    kidx = lax.broadcasted_iota(jnp.int32, (A_Q_HEADS * CHUNK, A_KV_WIDTH), 1)
    hi_half = kidx >= A_BAND - A_KV_WIDTH
    pad_k = jnp.zeros((2 * A_KV_WIDTH - A_BAND, A_KV_WIDTH), BF16)
    pad_r = lax.broadcasted_iota(jnp.int32, (2 * A_KV_WIDTH - A_BAND, 2 * A_KV_WIDTH), 0)
    pad_c = lax.broadcasted_iota(jnp.int32, (2 * A_KV_WIDTH - A_BAND, 2 * A_KV_WIDTH), 1)
    pad_v = jnp.where((pad_r == 0) & (pad_c >= A_KV_WIDTH), 1.0, 0.0).astype(BF16)

    def swa_chunk(c):
        r0 = c * CHUNK
        q = h_ref[r0:r0 + CHUNK, OFF_AQ:OFF_AQ + A_WIDTH]
        qc = [q[:, j * 128:(j + 1) * 128] for j in range(4)]
        zero = jnp.zeros_like(qc[0])
        lhs = jnp.concatenate([jnp.where(lo, qj, zero) for qj in qc] +
                              [jnp.where(lo, zero, qj) for qj in qc], axis=0).astype(BF16)
        band = slice(r0, r0 + A_BAND)
        s = _dot_nt(lhs, jnp.concatenate([k_ref[band, :], pad_k], axis=0))
        s0 = s[:, :A_KV_WIDTH]
        s1 = jnp.where(hi_half, sink_ref[...], s[:, A_KV_WIDTH:])
        if c < A_WIN_CHUNKS:
            n_masked = jnp.where(t == 0, (A_WIN_CHUNKS - c) * CHUNK, 0)
            s0 = jnp.where(kidx < n_masked, NEG_INF, s0)
        m = jnp.max(jnp.maximum(s0, s1), axis=-1, keepdims=True)
        p = jnp.concatenate([jnp.exp2(s0 - m), jnp.exp2(s1 - m)], axis=1).astype(BF16)
        pv = _dot(p, jnp.concatenate([v_ref[band, :], pad_v], axis=0))
        o = pv[:, :A_KV_WIDTH] / pv[:, A_KV_WIDTH:]
        ya = jnp.concatenate([jnp.where(lo, o[j * CHUNK:(j + 1) * CHUNK],
                                        o[(4 + j) * CHUNK:(5 + j) * CHUNK]) for j in range(4)], axis=1)
        ag = h_ref[r0:r0 + CHUNK, OFF_AG:OFF_AG + A_WIDTH]
        y_ref[r0:r0 + CHUNK, 0:A_WIDTH] = (ya * _silu(ag)).astype(BF16)

    def retention_group(rows):
        cos = cos_ref[rows, :]
        sin = sin_ref[rows, :]
        for h in range(B_HEADS):
            q = h_ref[rows, OFF_BQ + h * B_DIM:OFF_BQ + (h + 1) * B_DIM]
            k = h_ref[rows, OFF_BK + h * B_DIM:OFF_BK + (h + 1) * B_DIM]
            v = h_ref[rows, OFF_BV + h * B_DIM:OFF_BV + (h + 1) * B_DIM].astype(BF16)
            qr = q * cos + pltpu.roll(q, B_DIM // 2, 1) * sin
            kr = k * cos + pltpu.roll(k, B_DIM // 2, 1) * sin
            sc = _dot_nt(qr.astype(BF16), kr.astype(BF16)) * dmat_ref[h]
            state = state_ref[h]
            lhs = jnp.concatenate([sc.astype(BF16), (qr * qdec_ref[h]).astype(BF16)], axis=1)
            rhs = jnp.concatenate([v, state.astype(BF16)], axis=0)
            o = _dot(lhs, rhs)
            o = o * lax.rsqrt(jnp.mean(o * o, axis=-1, keepdims=True) + EPS)
            kd = (kr * kdec_ref[h]).T.astype(BF16)
            state_ref[h] = state * gdec_ref[h] + _dot(kd, v)
            bg = h_ref[rows, OFF_BG + h * B_DIM:OFF_BG + (h + 1) * B_DIM]
            y_ref[rows, A_WIDTH + h * B_DIM:A_WIDTH + (h + 1) * B_DIM] = (o * _silu(bg)).astype(BF16)

    blocks = [slice(s0, s0 + SUB0) for s0 in range(0, TS, SUB0)]
    xs = [x_ref[0, rows, :] for rows in blocks]
    for rows, x in zip(blocks, xs):
        xn = _rmsnorm(x, enorm_ref[...]).astype(BF16)
        h_ref[rows, :] = _dot(xn, win_ref[...])
        hist = slice(A_TAIL + rows.start, A_TAIL + rows.stop)
        k_ref[hist, :] = h_ref[rows, OFF_AK:OFF_AK + A_KV_WIDTH].astype(BF16)
        v_ref[hist, 0:A_KV_WIDTH] = h_ref[rows, OFF_AV:OFF_AV + A_KV_WIDTH].astype(BF16)
    x1s = []
    for rows, x in zip(blocks, xs):
        for c in range(rows.start // CHUNK, rows.stop // CHUNK):
            swa_chunk(c)
        for g0 in range(rows.start, rows.stop, B_GROUP):
            retention_group(slice(g0, g0 + B_GROUP))
        x1s.append(x + _dot(y_ref[rows, :], wout_ref[...]))
    x2s = _cross_attention_blocks(x1s, cnorm_ref, wq_ref, wo_ref, mkt_ref, mv_ref)
    for rows, x2 in zip(blocks, x2s):
        o_ref[0, rows, :] = x2


def _layer1_kernel(x_ref, onorm_ref, win_ref, convw_ref, convb_ref, wout_ref,
                   cnorm_ref, wq_ref, wo_ref, mkt_ref, mv_ref, fnorm_ref,
                   o_ref, h_ref, z_ref):
    t = pl.program_id(1)

    @pl.when(t == 0)
    def _():
        z_ref[0:C_PAD, :] = jnp.zeros((C_PAD, D_MODEL), F32)

    @pl.when(t > 0)
    def _():
        z_ref[0:C_PAD, :] = z_ref[TS:TS + C_PAD, :]

    blocks = [slice(s0, s0 + SUB1) for s0 in range(0, TS, SUB1)]
    xs = [x_ref[0, rows, :] for rows in blocks]
    for rows, x in zip(blocks, xs):
        xn = _rmsnorm(x, onorm_ref[...]).astype(BF16)
        h_ref[rows, :] = _dot(xn, win_ref[...])
    x1s = []
    for rows, x in zip(blocks, xs):
        zrows = lambda back: slice(C_PAD - back + rows.start, C_PAD - back + rows.stop)
        z = h_ref[rows, D_MODEL:2 * D_MODEL] * h_ref[rows, 2 * D_MODEL:3 * D_MODEL]
        z_ref[zrows(0), :] = z
        conv = convb_ref[...] + z * convw_ref[2:3, :]
        conv = conv + z_ref[zrows(1), :] * convw_ref[1:2, :]
        conv = conv + z_ref[zrows(2), :] * convw_ref[0:1, :]
        y = h_ref[rows, 0:D_MODEL] * conv * _silu(h_ref[rows, 3 * D_MODEL:4 * D_MODEL])
        x1s.append(x + _dot(y.astype(BF16), wout_ref[...]))
    x2s = _cross_attention_blocks(x1s, cnorm_ref, wq_ref, wo_ref, mkt_ref, mv_ref)
    for rows, x2 in zip(blocks, x2s):
        o_ref[0, rows, :] = _rmsnorm(x2, fnorm_ref[...])


def _const_spec(shape):
    return pl.BlockSpec(shape, lambda *_: (0,) * len(shape), pipeline_mode=pl.Buffered(1))


def _retention_tables(seq):
    f32 = np.float32
    half = B_DIM // 2
    inv = (1.0 / (f32(ROT_BASE) ** np.linspace(0.0, 1.0, half, dtype=f32))).astype(f32)
    ang = np.arange(seq, dtype=f32)[:, None] * inv[None, :]
    cos, sin = np.cos(ang), np.sin(ang)
    cos2 = np.concatenate([cos, cos], axis=1)
    sin2 = np.concatenate([-sin, sin], axis=1)
    log_gamma = np.log1p(-np.exp2(-5.0 - np.arange(B_HEADS, dtype=f32))).astype(f32)
    idx = np.arange(B_GROUP, dtype=f32)
    chunk_of = np.arange(B_GROUP) // CHUNK
    visible = (chunk_of[None, :] <= chunk_of[:, None]).astype(f32)
    scale = f32(B_DIM ** -0.5)
    dmat = np.exp(log_gamma[:, None, None] * np.abs(idx[:, None] - idx[None, :])) * visible * scale
    qdec = np.exp(log_gamma[:, None] * (idx + 1.0)[None, :])
    kdec = np.exp(log_gamma[:, None] * (B_GROUP - 1 - idx)[None, :]) * scale
    qdec = np.broadcast_to(qdec[:, :, None], (B_HEADS, B_GROUP, B_DIM))
    kdec = np.broadcast_to(kdec[:, :, None], (B_HEADS, B_GROUP, B_DIM))
    gdec = np.broadcast_to(np.exp(log_gamma * B_GROUP)[:, None, None], (B_HEADS, 1, B_DIM))
    return tuple(np.ascontiguousarray(a, dtype=f32) for a in (cos2, sin2, dmat, qdec, kdec, gdec))


def _forward(x, mem, e_norm, e_w_in, e_sink, e_w_out, o_norm, o_w_in, o_conv_w, o_conv_b,
             o_w_out, c_norm, c_mem_norm, c_wq, c_wkv, c_wo, final_norm):
    batch, seq, d = x.shape
    depth = c_wq.shape[0]
    assert d == D_MODEL and depth == 2 and seq % TS == 0 and mem.shape == (batch, MEM_LEN, D_MODEL)
    n_t = seq // TS
    params = pltpu.CompilerParams(dimension_semantics=("arbitrary", "arbitrary"),
                                  vmem_limit_bytes=VMEM_LIMIT_BYTES)

    mkt, mv = pl.pallas_call(
        _memkv_kernel,
        grid=(depth, batch),
        in_specs=[pl.BlockSpec((1, MEM_LEN, D_MODEL), lambda i, b: (b, 0, 0)),
                  pl.BlockSpec((1, 1, D_MODEL), lambda i, b: (i, 0, 0)),
                  pl.BlockSpec((1, D_MODEL, 2 * D_MODEL), lambda i, b: (i, 0, 0))],
        out_specs=[pl.BlockSpec((1, 1, D_MODEL, MEM_LEN), lambda i, b: (i, b, 0, 0)),
                   pl.BlockSpec((1, 1, MEM_LEN, D_MODEL), lambda i, b: (i, b, 0, 0))],
        out_shape=[jax.ShapeDtypeStruct((depth, batch, D_MODEL, MEM_LEN), BF16),
                   jax.ShapeDtypeStruct((depth, batch, MEM_LEN, D_MODEL), BF16)],
        compiler_params=params,
        name="memkv",
    )(mem, c_mem_norm.reshape(depth, 1, D_MODEL), c_wkv.astype(BF16))

    wq = (c_wq * (X_DIM ** -0.5 * LOG2E)).astype(BF16)
    wo = c_wo.astype(BF16)

    hd = np.arange(A_HEAD_DIM)
    perm = np.concatenate([np.concatenate([j * A_HEAD_DIM + hd, (4 + j) * A_HEAD_DIM + hd])
                           for j in range(4)])
    w0 = e_w_in[0]
    w_in0 = jnp.concatenate([w0[:, OFF_AQ + perm] * (A_HEAD_DIM ** -0.5 * LOG2E),
                             w0[:, OFF_AK:OFF_AG],
                             w0[:, OFF_AG + perm],
                             w0[:, OFF_BQ:]], axis=1).astype(BF16)
    w_out0 = jnp.concatenate([e_w_out[0][perm], e_w_out[0][A_WIDTH:]], axis=0).astype(BF16)
    cos2, sin2, dmat, qdec, kdec, gdec = _retention_tables(seq)
    sink_rows = jnp.repeat(e_sink[0].astype(F32) * LOG2E, CHUNK)
    sink_pad = jnp.where(jnp.arange(A_KV_WIDTH)[None, :] == A_BAND - A_KV_WIDTH,
                         sink_rows[:, None], NEG_INF).astype(F32)

    tile_spec = pl.BlockSpec((1, TS, D_MODEL), lambda b, t: (b, t, 0))
    mkt_spec = lambda i: pl.BlockSpec((None, 1, D_MODEL, MEM_LEN), lambda b, t: (i, b, 0, 0))
    mv_spec = lambda i: pl.BlockSpec((None, 1, MEM_LEN, D_MODEL), lambda b, t: (i, b, 0, 0))

    x1 = pl.pallas_call(
        _layer0_kernel,
        grid=(batch, n_t),
        in_specs=[tile_spec,
                  pl.BlockSpec((TS, B_DIM), lambda b, t: (t, 0)),
                  pl.BlockSpec((TS, B_DIM), lambda b, t: (t, 0)),
                  _const_spec((B_HEADS, B_GROUP, B_GROUP)),
                  _const_spec((B_HEADS, B_GROUP, B_DIM)),
                  _const_spec((B_HEADS, B_GROUP, B_DIM)),
                  _const_spec((B_HEADS, 1, B_DIM)),
                  _const_spec((A_Q_HEADS * CHUNK, A_KV_WIDTH)),
                  _const_spec((1, D_MODEL)),
                  _const_spec((D_MODEL, EVEN_IN)),
                  _const_spec((MIX_WIDTH, D_MODEL)),
                  _const_spec((1, D_MODEL)),
                  _const_spec((D_MODEL, D_MODEL)),
                  _const_spec((D_MODEL, D_MODEL)),
                  mkt_spec(0), mv_spec(0)],
        out_specs=tile_spec,
        out_shape=jax.ShapeDtypeStruct((batch, seq, D_MODEL), F32),
        scratch_shapes=[pltpu.VMEM((TS, EVEN_IN), F32),
                        pltpu.VMEM((TS, MIX_WIDTH), BF16),
                        pltpu.VMEM((A_TAIL + TS, A_KV_WIDTH), BF16),
                        pltpu.VMEM((A_TAIL + TS, 2 * A_KV_WIDTH), BF16),
                        pltpu.VMEM((B_HEADS, B_DIM, B_DIM), F32)],
        compiler_params=params,
        name="layer0",
    )(x, cos2, sin2, dmat, qdec, kdec, gdec, sink_pad,
      e_norm[0].reshape(1, D_MODEL), w_in0, w_out0,
      c_norm[0].reshape(1, D_MODEL), wq[0], wo[0], mkt, mv)

    out = pl.pallas_call(
        _layer1_kernel,
        grid=(batch, n_t),
        in_specs=[tile_spec,
                  _const_spec((1, D_MODEL)),
                  _const_spec((D_MODEL, 4 * D_MODEL)),
                  _const_spec((C_CONV, D_MODEL)),
                  _const_spec((1, D_MODEL)),
                  _const_spec((D_MODEL, D_MODEL)),
                  _const_spec((1, D_MODEL)),
                  _const_spec((D_MODEL, D_MODEL)),
                  _const_spec((D_MODEL, D_MODEL)),
                  mkt_spec(1), mv_spec(1),
                  _const_spec((1, D_MODEL))],
        out_specs=tile_spec,
        out_shape=jax.ShapeDtypeStruct((batch, seq, D_MODEL), F32),
        scratch_shapes=[pltpu.VMEM((TS, 4 * D_MODEL), F32),
                        pltpu.VMEM((C_PAD + TS, D_MODEL), F32)],
        compiler_params=params,
        name="layer1",
    )(x1, o_norm[0].reshape(1, D_MODEL), o_w_in[0].astype(BF16), o_conv_w[0],
      o_conv_b[0].reshape(1, D_MODEL), o_w_out[0].astype(BF16),
      c_norm[1].reshape(1, D_MODEL), wq[1], wo[1], mkt, mv, final_norm.reshape(1, D_MODEL))
    return x1, out


def kernel(x, mem, e_norm, e_w_in, e_sink, e_w_out, o_norm, o_w_in, o_conv_w, o_conv_b,
           o_w_out, c_norm, c_mem_norm, c_wq, c_wkv, c_wo, final_norm):
    return _forward(x, mem, e_norm, e_w_in, e_sink, e_w_out, o_norm, o_w_in, o_conv_w, o_conv_b,
                    o_w_out, c_norm, c_mem_norm, c_wq, c_wkv, c_wo, final_norm)[1]
```

```python
import jax
import jax.numpy as jnp
import numpy as np
from jax import lax
from jax.experimental import pallas as pl
from jax.experimental.pallas import tpu as pltpu

F32 = jnp.float32
BF16 = jnp.bfloat16

D_MODEL = 1024
CHUNK = 64
MEM_LEN = 256
EPS = 1e-6
NEG_INF = -1e30
LOG2E = 1.4426950408889634

A_Q_HEADS = 8
A_KV_HEADS = 2
A_HEAD_DIM = 64
A_WIN_CHUNKS = 2
A_WIDTH = A_Q_HEADS * A_HEAD_DIM
A_KV_WIDTH = A_KV_HEADS * A_HEAD_DIM
A_BAND = (A_WIN_CHUNKS + 1) * CHUNK
A_TAIL = A_WIN_CHUNKS * CHUNK

B_HEADS = 4
B_DIM = 128
B_WIDTH = B_HEADS * B_DIM
ROT_BASE = 10000.0
B_GROUP = 256

C_CONV = 3
C_PAD = 8

X_HEADS = 4
X_DIM = D_MODEL // X_HEADS

OFF_AQ = 0
OFF_AK = OFF_AQ + A_WIDTH
OFF_AV = OFF_AK + A_KV_WIDTH
OFF_AG = OFF_AV + A_KV_WIDTH
OFF_BQ = OFF_AG + A_WIDTH
OFF_BK = OFF_BQ + B_WIDTH
OFF_BV = OFF_BK + B_WIDTH
OFF_BG = OFF_BV + B_WIDTH
EVEN_IN = OFF_BG + B_WIDTH
MIX_WIDTH = A_WIDTH + B_WIDTH

TS = 1024
SUB0 = 256
SUB1 = 256
VMEM_LIMIT_BYTES = 56 * 1024 * 1024


def _rmsnorm(x, g):
    return x * lax.rsqrt(jnp.mean(x * x, axis=-1, keepdims=True) + EPS) * g


def _silu(x):
    return x * jax.nn.sigmoid(x)


def _dot(a, b):
    return jnp.dot(a, b, preferred_element_type=F32)


def _dot_nt(a, b):
    return lax.dot_general(a, b, (((1,), (1,)), ((), ())), preferred_element_type=F32)


def _memkv_kernel(mem_ref, g_ref, w_ref, kt_ref, v_ref):
    mn = _rmsnorm(mem_ref[0], g_ref[0]).astype(BF16)
    kv = _dot(mn, w_ref[0])
    kt_ref[0, 0] = kv[:, :D_MODEL].T.astype(BF16)
    v_ref[0, 0] = kv[:, D_MODEL:].astype(BF16)


def _cross_attention_blocks(x1s, cnorm_ref, wq_ref, wo_ref, mkt_ref, mv_ref):
    qs = [_dot(_rmsnorm(x1, cnorm_ref[...]).astype(BF16), wq_ref[...]).astype(BF16) for x1 in x1s]
    os = []
    for q in qs:
        outs = []
        for h in range(X_HEADS):
            cs = slice(h * X_DIM, (h + 1) * X_DIM)
            s = _dot(q[:, cs], mkt_ref[0, cs, :])
            m = jnp.max(s, axis=-1, keepdims=True)
            p = jnp.exp2(s - m)
            l = jnp.sum(p, axis=-1, keepdims=True)
            outs.append((_dot(p.astype(BF16), mv_ref[0, :, cs]) / l).astype(BF16))
        os.append(jnp.concatenate(outs, axis=1))
    return [x1 + _dot(o, wo_ref[...]) for x1, o in zip(x1s, os)]


def _layer0_kernel(x_ref, cos_ref, sin_ref, dmat_ref, qdec_ref, kdec_ref, gdec_ref, sink_ref,
                   enorm_ref, win_ref, wout_ref, cnorm_ref, wq_ref, wo_ref, mkt_ref, mv_ref,
                   o_ref, h_ref, y_ref, k_ref, v_ref, state_ref):
    t = pl.program_id(1)

    @pl.when(t == 0)
    def _():
        zk = jnp.zeros((A_TAIL, A_KV_WIDTH), BF16)
        k_ref[0:A_TAIL, :] = zk
        v_ref[0:A_TAIL, 0:A_KV_WIDTH] = zk
        v_ref[:, A_KV_WIDTH:] = jnp.ones((A_TAIL + TS, A_KV_WIDTH), BF16)
        state_ref[...] = jnp.zeros_like(state_ref)

    @pl.when(t > 0)
    def _():
        k_ref[0:A_TAIL, :] = k_ref[TS:TS + A_TAIL, :]
        v_ref[0:A_TAIL, 0:A_KV_WIDTH] = v_ref[TS:TS + A_TAIL, 0:A_KV_WIDTH]

    lane = lax.broadcasted_iota(jnp.int32, (CHUNK, A_KV_WIDTH), 1)
    lo = lane < A_HEAD_DIM
    kidx = lax.broadcasted_iota(jnp.int32, (A_Q_HEADS * CHUNK, A_KV_WIDTH), 1)
    hi_half = kidx >= A_BAND - A_KV_WIDTH
    pad_k = jnp.zeros((2 * A_KV_WIDTH - A_BAND, A_KV_WIDTH), BF16)
    pad_r = lax.broadcasted_iota(jnp.int32, (2 * A_KV_WIDTH - A_BAND, 2 * A_KV_WIDTH), 0)
    pad_c = lax.broadcasted_iota(jnp.int32, (2 * A_KV_WIDTH - A_BAND, 2 * A_KV_WIDTH), 1)
    pad_v = jnp.where((pad_r == 0) & (pad_c >= A_KV_WIDTH), 1.0, 0.0).astype(BF16)

    def swa_chunk(c):
        r0 = c * CHUNK
        q = h_ref[r0:r0 + CHUNK, OFF_AQ:OFF_AQ + A_WIDTH]
        qc = [q[:, j * 128:(j + 1) * 128] for j in range(4)]
        zero = jnp.zeros_like(qc[0])
        lhs = jnp.concatenate([jnp.where(lo, qj, zero) for qj in qc] +
                              [jnp.where(lo, zero, qj) for qj in qc], axis=0).astype(BF16)
        band = slice(r0, r0 + A_BAND)
        s = _dot_nt(lhs, jnp.concatenate([k_ref[band, :], pad_k], axis=0))
        s0 = s[:, :A_KV_WIDTH]
        s1 = jnp.where(hi_half, sink_ref[...], s[:, A_KV_WIDTH:])
        if c < A_WIN_CHUNKS:
            n_masked = jnp.where(t == 0, (A_WIN_CHUNKS - c) * CHUNK, 0)
            s0 = jnp.where(kidx < n_masked, NEG_INF, s0)
        m = jnp.max(jnp.maximum(s0, s1), axis=-1, keepdims=True)
        p = jnp.concatenate([jnp.exp2(s0 - m), jnp.exp2(s1 - m)], axis=1).astype(BF16)
        pv = _dot(p, jnp.concatenate([v_ref[band, :], pad_v], axis=0))
        o = pv[:, :A_KV_WIDTH] / pv[:, A_KV_WIDTH:]
        ya = jnp.concatenate([jnp.where(lo, o[j * CHUNK:(j + 1) * CHUNK],
                                        o[(4 + j) * CHUNK:(5 + j) * CHUNK]) for j in range(4)], axis=1)
        ag = h_ref[r0:r0 + CHUNK, OFF_AG:OFF_AG + A_WIDTH]
        y_ref[r0:r0 + CHUNK, 0:A_WIDTH] = (ya * _silu(ag)).astype(BF16)

    def retention_group(rows):
        cos = cos_ref[rows, :]
        sin = sin_ref[rows, :]
        for h in range(B_HEADS):
            q = h_ref[rows, OFF_BQ + h * B_DIM:OFF_BQ + (h + 1) * B_DIM]
            k = h_ref[rows, OFF_BK + h * B_DIM:OFF_BK + (h + 1) * B_DIM]
            v = h_ref[rows, OFF_BV + h * B_DIM:OFF_BV + (h + 1) * B_DIM].astype(BF16)
            qr = q * cos + pltpu.roll(q, B_DIM // 2, 1) * sin
            kr = k * cos + pltpu.roll(k, B_DIM // 2, 1) * sin
            sc = _dot_nt(qr.astype(BF16), kr.astype(BF16)) * dmat_ref[h]
            state = state_ref[h]
            lhs = jnp.concatenate([sc.astype(BF16), (qr * qdec_ref[h]).astype(BF16)], axis=1)
            rhs = jnp.concatenate([v, state.astype(BF16)], axis=0)
            o = _dot(lhs, rhs)
            o = o * lax.rsqrt(jnp.mean(o * o, axis=-1, keepdims=True) + EPS)
            kd = (kr * kdec_ref[h]).T.astype(BF16)
            state_ref[h] = state * gdec_ref[h] + _dot(kd, v)
            bg = h_ref[rows, OFF_BG + h * B_DIM:OFF_BG + (h + 1) * B_DIM]
            y_ref[rows, A_WIDTH + h * B_DIM:A_WIDTH + (h + 1) * B_DIM] = (o * _silu(bg)).astype(BF16)

    blocks = [slice(s0, s0 + SUB0) for s0 in range(0, TS, SUB0)]
    xs = [x_ref[0, rows, :] for rows in blocks]
    for rows, x in zip(blocks, xs):
        xn = _rmsnorm(x, enorm_ref[...]).astype(BF16)
        h_ref[rows, :] = _dot(xn, win_ref[...])
        hist = slice(A_TAIL + rows.start, A_TAIL + rows.stop)
        k_ref[hist, :] = h_ref[rows, OFF_AK:OFF_AK + A_KV_WIDTH].astype(BF16)
        v_ref[hist, 0:A_KV_WIDTH] = h_ref[rows, OFF_AV:OFF_AV + A_KV_WIDTH].astype(BF16)
    x1s = []
    for rows, x in zip(blocks, xs):
        for c in range(rows.start // CHUNK, rows.stop // CHUNK):
            swa_chunk(c)
        for g0 in range(rows.start, rows.stop, B_GROUP):
            retention_group(slice(g0, g0 + B_GROUP))
        x1s.append(x + _dot(y_ref[rows, :], wout_ref[...]))
    x2s = _cross_attention_blocks(x1s, cnorm_ref, wq_ref, wo_ref, mkt_ref, mv_ref)
    for rows, x2 in zip(blocks, x2s):
        o_ref[0, rows, :] = x2


def _layer1_kernel(x_ref, onorm_ref, win_ref, convw_ref, convb_ref, wout_ref,
                   cnorm_ref, wq_ref, wo_ref, mkt_ref, mv_ref, fnorm_ref,
                   o_ref, h_ref, z_ref):
    t = pl.program_id(1)

    @pl.when(t == 0)
    def _():
        z_ref[0:C_PAD, :] = jnp.zeros((C_PAD, D_MODEL), F32)

    @pl.when(t > 0)
    def _():
        z_ref[0:C_PAD, :] = z_ref[TS:TS + C_PAD, :]

    blocks = [slice(s0, s0 + SUB1) for s0 in range(0, TS, SUB1)]
    xs = [x_ref[0, rows, :] for rows in blocks]
    for rows, x in zip(blocks, xs):
        xn = _rmsnorm(x, onorm_ref[...]).astype(BF16)
        h_ref[rows, :] = _dot(xn, win_ref[...])
    x1s = []
    for rows, x in zip(blocks, xs):
        zrows = lambda back: slice(C_PAD - back + rows.start, C_PAD - back + rows.stop)
        z = h_ref[rows, D_MODEL:2 * D_MODEL] * h_ref[rows, 2 * D_MODEL:3 * D_MODEL]
        z_ref[zrows(0), :] = z
        conv = convb_ref[...] + z * convw_ref[2:3, :]
        conv = conv + z_ref[zrows(1), :] * convw_ref[1:2, :]
        conv = conv + z_ref[zrows(2), :] * convw_ref[0:1, :]
        y = h_ref[rows, 0:D_MODEL] * conv * _silu(h_ref[rows, 3 * D_MODEL:4 * D_MODEL])
        x1s.append(x + _dot(y.astype(BF16), wout_ref[...]))
    x2s = _cross_attention_blocks(x1s, cnorm_ref, wq_ref, wo_ref, mkt_ref, mv_ref)
    for rows, x2 in zip(blocks, x2s):
        o_ref[0, rows, :] = _rmsnorm(x2, fnorm_ref[...])


def _const_spec(shape):
    return pl.BlockSpec(shape, lambda *_: (0,) * len(shape), pipeline_mode=pl.Buffered(1))


def _retention_tables(seq):
    f32 = np.float32
    half = B_DIM // 2
    inv = (1.0 / (f32(ROT_BASE) ** np.linspace(0.0, 1.0, half, dtype=f32))).astype(f32)
    ang = np.arange(seq, dtype=f32)[:, None] * inv[None, :]
    cos, sin = np.cos(ang), np.sin(ang)
    cos2 = np.concatenate([cos, cos], axis=1)
    sin2 = np.concatenate([-sin, sin], axis=1)
    log_gamma = np.log1p(-np.exp2(-5.0 - np.arange(B_HEADS, dtype=f32))).astype(f32)
    idx = np.arange(B_GROUP, dtype=f32)
    chunk_of = np.arange(B_GROUP) // CHUNK
    visible = (chunk_of[None, :] <= chunk_of[:, None]).astype(f32)
    scale = f32(B_DIM ** -0.5)
    dmat = np.exp(log_gamma[:, None, None] * np.abs(idx[:, None] - idx[None, :])) * visible * scale
    qdec = np.exp(log_gamma[:, None] * (idx + 1.0)[None, :])
    kdec = np.exp(log_gamma[:, None] * (B_GROUP - 1 - idx)[None, :]) * scale
    qdec = np.broadcast_to(qdec[:, :, None], (B_HEADS, B_GROUP, B_DIM))
    kdec = np.broadcast_to(kdec[:, :, None], (B_HEADS, B_GROUP, B_DIM))
    gdec = np.broadcast_to(np.exp(log_gamma * B_GROUP)[:, None, None], (B_HEADS, 1, B_DIM))
    return tuple(np.ascontiguousarray(a, dtype=f32) for a in (cos2, sin2, dmat, qdec, kdec, gdec))


def _forward(x, mem, e_norm, e_w_in, e_sink, e_w_out, o_norm, o_w_in, o_conv_w, o_conv_b,
             o_w_out, c_norm, c_mem_norm, c_wq, c_wkv, c_wo, final_norm):
    batch, seq, d = x.shape
    depth = c_wq.shape[0]
    assert d == D_MODEL and depth == 2 and seq % TS == 0 and mem.shape == (batch, MEM_LEN, D_MODEL)
    n_t = seq // TS
    params = pltpu.CompilerParams(dimension_semantics=("arbitrary", "arbitrary"),
                                  vmem_limit_bytes=VMEM_LIMIT_BYTES)

    mkt, mv = pl.pallas_call(
        _memkv_kernel,
        grid=(depth, batch),
        in_specs=[pl.BlockSpec((1, MEM_LEN, D_MODEL), lambda i, b: (b, 0, 0)),
                  pl.BlockSpec((1, 1, D_MODEL), lambda i, b: (i, 0, 0)),
                  pl.BlockSpec((1, D_MODEL, 2 * D_MODEL), lambda i, b: (i, 0, 0))],
        out_specs=[pl.BlockSpec((1, 1, D_MODEL, MEM_LEN), lambda i, b: (i, b, 0, 0)),
                   pl.BlockSpec((1, 1, MEM_LEN, D_MODEL), lambda i, b: (i, b, 0, 0))],
        out_shape=[jax.ShapeDtypeStruct((depth, batch, D_MODEL, MEM_LEN), BF16),
                   jax.ShapeDtypeStruct((depth, batch, MEM_LEN, D_MODEL), BF16)],
        compiler_params=params,
        name="memkv",
    )(mem, c_mem_norm.reshape(depth, 1, D_MODEL), c_wkv.astype(BF16))

    wq = (c_wq * (X_DIM ** -0.5 * LOG2E)).astype(BF16)
    wo = c_wo.astype(BF16)

    head_order = [h for j in range(4) for h in (j, 4 + j)]

    def regroup(w, axis):
        heads = jnp.split(w, A_Q_HEADS, axis=axis)
        return jnp.concatenate([heads[h] for h in head_order], axis=axis)

    w0 = e_w_in[0]
    w_in0 = jnp.concatenate([regroup(w0[:, OFF_AQ:OFF_AK], 1) * (A_HEAD_DIM ** -0.5 * LOG2E),
                             w0[:, OFF_AK:OFF_AG],
                             regroup(w0[:, OFF_AG:OFF_BQ], 1),
                             w0[:, OFF_BQ:]], axis=1).astype(BF16)
    w_out0 = jnp.concatenate([regroup(e_w_out[0][:A_WIDTH], 0), e_w_out[0][A_WIDTH:]],
                             axis=0).astype(BF16)
    cos2, sin2, dmat, qdec, kdec, gdec = _retention_tables(seq)
    sink_rows = jnp.broadcast_to((e_sink[0].astype(F32) * LOG2E)[:, None, None],
                                 (A_Q_HEADS, CHUNK, A_KV_WIDTH)).reshape(A_Q_HEADS * CHUNK, A_KV_WIDTH)
    sink_pad = jnp.where(np.arange(A_KV_WIDTH)[None, :] == A_BAND - A_KV_WIDTH,
                         sink_rows, NEG_INF).astype(F32)

    tile_spec = pl.BlockSpec((1, TS, D_MODEL), lambda b, t: (b, t, 0))
    mkt_spec = lambda i: pl.BlockSpec((None, 1, D_MODEL, MEM_LEN), lambda b, t: (i, b, 0, 0))
    mv_spec = lambda i: pl.BlockSpec((None, 1, MEM_LEN, D_MODEL), lambda b, t: (i, b, 0, 0))

    x1 = pl.pallas_call(
        _layer0_kernel,
        grid=(batch, n_t),
        in_specs=[tile_spec,
                  pl.BlockSpec((TS, B_DIM), lambda b, t: (t, 0)),
                  pl.BlockSpec((TS, B_DIM), lambda b, t: (t, 0)),
                  _const_spec((B_HEADS, B_GROUP, B_GROUP)),
                  _const_spec((B_HEADS, B_GROUP, B_DIM)),
                  _const_spec((B_HEADS, B_GROUP, B_DIM)),
                  _const_spec((B_HEADS, 1, B_DIM)),
                  _const_spec((A_Q_HEADS * CHUNK, A_KV_WIDTH)),
                  _const_spec((1, D_MODEL)),
                  _const_spec((D_MODEL, EVEN_IN)),
                  _const_spec((MIX_WIDTH, D_MODEL)),
                  _const_spec((1, D_MODEL)),
                  _const_spec((D_MODEL, D_MODEL)),
                  _const_spec((D_MODEL, D_MODEL)),
                  mkt_spec(0), mv_spec(0)],
        out_specs=tile_spec,
        out_shape=jax.ShapeDtypeStruct((batch, seq, D_MODEL), F32),
        scratch_shapes=[pltpu.VMEM((TS, EVEN_IN), F32),
                        pltpu.VMEM((TS, MIX_WIDTH), BF16),
                        pltpu.VMEM((A_TAIL + TS, A_KV_WIDTH), BF16),
                        pltpu.VMEM((A_TAIL + TS, 2 * A_KV_WIDTH), BF16),
                        pltpu.VMEM((B_HEADS, B_DIM, B_DIM), F32)],
        compiler_params=params,
        name="layer0",
    )(x, cos2, sin2, dmat, qdec, kdec, gdec, sink_pad,
      e_norm[0].reshape(1, D_MODEL), w_in0, w_out0,
      c_norm[0].reshape(1, D_MODEL), wq[0], wo[0], mkt, mv)

    out = pl.pallas_call(
        _layer1_kernel,
        grid=(batch, n_t),
        in_specs=[tile_spec,
                  _const_spec((1, D_MODEL)),
                  _const_spec((D_MODEL, 4 * D_MODEL)),
                  _const_spec((C_CONV, D_MODEL)),
                  _const_spec((1, D_MODEL)),
                  _const_spec((D_MODEL, D_MODEL)),
                  _const_spec((1, D_MODEL)),
                  _const_spec((D_MODEL, D_MODEL)),
                  _const_spec((D_MODEL, D_MODEL)),
                  mkt_spec(1), mv_spec(1),
                  _const_spec((1, D_MODEL))],
        out_specs=tile_spec,
        out_shape=jax.ShapeDtypeStruct((batch, seq, D_MODEL), F32),
        scratch_shapes=[pltpu.VMEM((TS, 4 * D_MODEL), F32),
                        pltpu.VMEM((C_PAD + TS, D_MODEL), F32)],
        compiler_params=params,
        name="layer1",
    )(x1, o_norm[0].reshape(1, D_MODEL), o_w_in[0].astype(BF16), o_conv_w[0],
      o_conv_b[0].reshape(1, D_MODEL), o_w_out[0].astype(BF16),
      c_norm[1].reshape(1, D_MODEL), wq[1], wo[1], mkt, mv, final_norm.reshape(1, D_MODEL))
    return x1, out


def kernel(x, mem, e_norm, e_w_in, e_sink, e_w_out, o_norm, o_w_in, o_conv_w, o_conv_b,
           o_w_out, c_norm, c_mem_norm, c_wq, c_wkv, c_wo, final_norm):
    return _forward(x, mem, e_norm, e_w_in, e_sink, e_w_out, o_norm, o_w_in, o_conv_w, o_conv_b,
                    o_w_out, c_norm, c_mem_norm, c_wq, c_wkv, c_wo, final_norm)[1]
```

```python
import jax
import jax.numpy as jnp
import numpy as np
from jax import lax
from jax.experimental import pallas as pl
from jax.experimental.pallas import tpu as pltpu

F32 = jnp.float32
BF16 = jnp.bfloat16

D_MODEL = 1024
CHUNK = 64
MEM_LEN = 256
EPS = 1e-6
NEG_INF = -1e30
LOG2E = 1.4426950408889634

A_Q_HEADS = 8
A_KV_HEADS = 2
A_HEAD_DIM = 64
A_WIN_CHUNKS = 2
A_WIDTH = A_Q_HEADS * A_HEAD_DIM
A_KV_WIDTH = A_KV_HEADS * A_HEAD_DIM
A_BAND = (A_WIN_CHUNKS + 1) * CHUNK
A_TAIL = A_WIN_CHUNKS * CHUNK
A_SCALE = A_HEAD_DIM ** -0.5 * LOG2E

B_HEADS = 4
B_DIM = 128
B_WIDTH = B_HEADS * B_DIM
ROT_BASE = 10000.0
B_GROUP = 256

C_CONV = 3
C_PAD = 8

X_HEADS = 4
X_DIM = D_MODEL // X_HEADS
X_SCALE = X_DIM ** -0.5 * LOG2E

OFF_AQ = 0
OFF_AK = OFF_AQ + A_WIDTH
OFF_AV = OFF_AK + A_KV_WIDTH
OFF_AG = OFF_AV + A_KV_WIDTH
OFF_BQ = OFF_AG + A_WIDTH
OFF_BK = OFF_BQ + B_WIDTH
OFF_BV = OFF_BK + B_WIDTH
OFF_BG = OFF_BV + B_WIDTH
EVEN_IN = OFF_BG + B_WIDTH
MIX_WIDTH = A_WIDTH + B_WIDTH

TS = 1024
SUB0 = 256
SUB1 = 256
VMEM_LIMIT_BYTES = 56 * 1024 * 1024


def _rmsnorm(x, g):
    return x * lax.rsqrt(jnp.mean(x * x, axis=-1, keepdims=True) + EPS) * g


def _silu(x):
    return x * jax.nn.sigmoid(x)


def _dot(a, b):
    return jnp.dot(a, b, preferred_element_type=F32)


def _dot_nt(a, b):
    return lax.dot_general(a, b, (((1,), (1,)), ((), ())), preferred_element_type=F32)


def _cast_blocks(pairs):
    for src_ref, dst_ref in pairs:
        dst_ref[...] = src_ref[...].astype(BF16)


def _memkv_kernel(mem_ref, g_ref, wkv_ref, win_ref, wout_ref, wq_ref, wo_ref,
                  kt_ref, v_ref, win_bf_ref, wout_bf_ref, wq_bf_ref, wo_bf_ref, wkv_bf_ref):
    @pl.when(pl.program_id(1) == 0)
    def _():
        wkv_bf_ref[...] = wkv_ref[...].astype(BF16)

    mn = _rmsnorm(mem_ref[0], g_ref[0]).astype(BF16)
    kv = _dot(mn, wkv_bf_ref[...])
    kt_ref[0, 0] = (kv[:, :D_MODEL] * X_SCALE).T.astype(BF16)
    v_ref[0, 0] = kv[:, D_MODEL:].astype(BF16)
    _cast_blocks([(win_ref, win_bf_ref), (wout_ref, wout_bf_ref),
                  (wq_ref, wq_bf_ref), (wo_ref, wo_bf_ref)])


def _cross_attention_blocks(x1s, cnorm_ref, wq_ref, wo_ref, mkt_ref, mv_ref):
    qs = [_dot(_rmsnorm(x1, cnorm_ref[...]).astype(BF16), wq_ref[...]).astype(BF16) for x1 in x1s]
    os = []
    for q in qs:
        outs = []
        for h in range(X_HEADS):
            cs = slice(h * X_DIM, (h + 1) * X_DIM)
            s = _dot(q[:, cs], mkt_ref[0, cs, :])
            m = jnp.max(s, axis=-1, keepdims=True)
            p = jnp.exp2(s - m)
            l = jnp.sum(p, axis=-1, keepdims=True)
            outs.append((_dot(p.astype(BF16), mv_ref[0, :, cs]) / l).astype(BF16))
        os.append(jnp.concatenate(outs, axis=1))
    return [x1 + _dot(o, wo_ref[...]) for x1, o in zip(x1s, os)]


def _layer0_kernel(x_ref, cos_ref, sin_ref, dmat_ref, qdec_ref, kdec_ref, gdec_ref, sink_ref,
                   enorm_ref, win_ref, wout_ref, cnorm_ref, wq_ref, wo_ref, mkt_ref, mv_ref,
                   nwin_ref, nwout_ref, nwq_ref, nwo_ref,
                   o_ref, nwin_bf_ref, nwout_bf_ref, nwq_bf_ref, nwo_bf_ref,
                   h_ref, y_ref, k_ref, v_ref, state_ref):
    t = pl.program_id(1)
    _cast_blocks([(nwin_ref, nwin_bf_ref), (nwout_ref, nwout_bf_ref),
                  (nwq_ref, nwq_bf_ref), (nwo_ref, nwo_bf_ref)])

    @pl.when(t == 0)
    def _():
        k_ref[:, 0:A_TAIL, :] = jnp.zeros((A_KV_HEADS, A_TAIL, A_KV_WIDTH), BF16)
        v_ref[:, 0:A_TAIL, 0:A_KV_WIDTH] = jnp.zeros((A_KV_HEADS, A_TAIL, A_KV_WIDTH), BF16)
        v_ref[:, :, A_KV_WIDTH:] = jnp.ones((A_KV_HEADS, A_TAIL + TS, A_KV_WIDTH), BF16)
        state_ref[...] = jnp.zeros_like(state_ref)

    @pl.when(t > 0)
    def _():
        k_ref[:, 0:A_TAIL, :] = k_ref[:, TS:TS + A_TAIL, :]
        v_ref[:, 0:A_TAIL, 0:A_KV_WIDTH] = v_ref[:, TS:TS + A_TAIL, 0:A_KV_WIDTH]

    lo = lax.broadcasted_iota(jnp.int32, (CHUNK, A_KV_WIDTH), 1) < A_HEAD_DIM
    lo_blk = lax.broadcasted_iota(jnp.int32, (SUB0, A_KV_WIDTH), 1) < A_HEAD_DIM
    kidx = lax.broadcasted_iota(jnp.int32, (4 * CHUNK, A_KV_WIDTH), 1)
    hi_half = kidx >= A_BAND - A_KV_WIDTH
    pad_k = jnp.zeros((2 * A_KV_WIDTH - A_BAND, A_KV_WIDTH), BF16)
    pad_r = lax.broadcasted_iota(jnp.int32, (2 * A_KV_WIDTH - A_BAND, 2 * A_KV_WIDTH), 0)
    pad_c = lax.broadcasted_iota(jnp.int32, (2 * A_KV_WIDTH - A_BAND, 2 * A_KV_WIDTH), 1)
    pad_v = jnp.where((pad_r == 0) & (pad_c >= A_KV_WIDTH), 1.0, 0.0).astype(BF16)

    def store_kv(rows):
        hist = slice(A_TAIL + rows.start, A_TAIL + rows.stop)
        kk = h_ref[rows, OFF_AK:OFF_AK + A_KV_WIDTH] * A_SCALE
        vv = h_ref[rows, OFF_AV:OFF_AV + A_KV_WIDTH]
        kr = pltpu.roll(kk, A_HEAD_DIM, 1)
        vr = pltpu.roll(vv, A_HEAD_DIM, 1)
        k_ref[0, hist, :] = jnp.where(lo_blk, kk, kr).astype(BF16)
        k_ref[1, hist, :] = jnp.where(lo_blk, kr, kk).astype(BF16)
        v_ref[0, hist, 0:A_KV_WIDTH] = jnp.where(lo_blk, vv, vr).astype(BF16)
        v_ref[1, hist, 0:A_KV_WIDTH] = jnp.where(lo_blk, vr, vv).astype(BF16)

    def swa_chunk(c):
        r0 = c * CHUNK
        band = slice(r0, r0 + A_BAND)
        q = h_ref[r0:r0 + CHUNK, OFF_AQ:OFF_AQ + A_WIDTH]
        cols = []
        for g in range(A_KV_HEADS):
            qa, qb = q[:, (2 * g) * 128:(2 * g + 1) * 128], q[:, (2 * g + 1) * 128:(2 * g + 2) * 128]
            zero = jnp.zeros_like(qa)
            lhs = jnp.concatenate([jnp.where(lo, qa, zero), jnp.where(lo, zero, qa),
                                   jnp.where(lo, qb, zero), jnp.where(lo, zero, qb)],
                                  axis=0).astype(BF16)
            s = _dot_nt(lhs, jnp.concatenate([k_ref[g, band, :], pad_k], axis=0))
            s0 = s[:, :A_KV_WIDTH]
            s1 = jnp.where(hi_half, sink_ref[g * 4 * CHUNK:(g + 1) * 4 * CHUNK, :],
                           s[:, A_KV_WIDTH:])
            if c < A_WIN_CHUNKS:
                n_masked = jnp.where(t == 0, (A_WIN_CHUNKS - c) * CHUNK, 0)
                s0 = jnp.where(kidx < n_masked, NEG_INF, s0)
            m = jnp.max(jnp.maximum(s0, s1), axis=-1, keepdims=True)
            p = jnp.concatenate([jnp.exp2(s0 - m), jnp.exp2(s1 - m)], axis=1).astype(BF16)
            pv = _dot(p, jnp.concatenate([v_ref[g, band, :], pad_v], axis=0))
            o = pv[:, :A_KV_WIDTH] / pv[:, A_KV_WIDTH:]
            cols += [jnp.where(lo, o[0:CHUNK], o[CHUNK:2 * CHUNK]),
                     jnp.where(lo, o[2 * CHUNK:3 * CHUNK], o[3 * CHUNK:4 * CHUNK])]
        ya = jnp.concatenate(cols, axis=1)
        ag = h_ref[r0:r0 + CHUNK, OFF_AG:OFF_AG + A_WIDTH]
        y_ref[r0:r0 + CHUNK, 0:A_WIDTH] = (ya * _silu(ag)).astype(BF16)

    def retention_group(rows):
        cos = cos_ref[rows, :]
        sin = sin_ref[rows, :]
        for h in range(B_HEADS):
            q = h_ref[rows, OFF_BQ + h * B_DIM:OFF_BQ + (h + 1) * B_DIM]
            k = h_ref[rows, OFF_BK + h * B_DIM:OFF_BK + (h + 1) * B_DIM]
            v = h_ref[rows, OFF_BV + h * B_DIM:OFF_BV + (h + 1) * B_DIM].astype(BF16)
            qr = q * cos + pltpu.roll(q, B_DIM // 2, 1) * sin
            kr = k * cos + pltpu.roll(k, B_DIM // 2, 1) * sin
            sc = _dot_nt(qr.astype(BF16), kr.astype(BF16)) * dmat_ref[h]
            state = state_ref[h]
            lhs = jnp.concatenate([sc.astype(BF16), (qr * qdec_ref[h]).astype(BF16)], axis=1)
            rhs = jnp.concatenate([v, state.astype(BF16)], axis=0)
            o = _dot(lhs, rhs)
            o = o * lax.rsqrt(jnp.mean(o * o, axis=-1, keepdims=True) + EPS)
            kd = (kr * kdec_ref[h]).T.astype(BF16)
            state_ref[h] = state * gdec_ref[h] + _dot(kd, v)
            bg = h_ref[rows, OFF_BG + h * B_DIM:OFF_BG + (h + 1) * B_DIM]
            y_ref[rows, A_WIDTH + h * B_DIM:A_WIDTH + (h + 1) * B_DIM] = (o * _silu(bg)).astype(BF16)

    blocks = [slice(s0, s0 + SUB0) for s0 in range(0, TS, SUB0)]
    xs = [x_ref[0, rows, :] for rows in blocks]
    for rows, x in zip(blocks, xs):
        xn = _rmsnorm(x, enorm_ref[...]).astype(BF16)
        h_ref[rows, :] = _dot(xn, win_ref[...])
        store_kv(rows)
    x1s = []
    for rows, x in zip(blocks, xs):
        for c in range(rows.start // CHUNK, rows.stop // CHUNK):
            swa_chunk(c)
        for g0 in range(rows.start, rows.stop, B_GROUP):
            retention_group(slice(g0, g0 + B_GROUP))
        x1s.append(x + _dot(y_ref[rows, :], wout_ref[...]))
    x2s = _cross_attention_blocks(x1s, cnorm_ref, wq_ref, wo_ref, mkt_ref, mv_ref)
    for rows, x2 in zip(blocks, x2s):
        o_ref[0, rows, :] = x2


def _layer1_kernel(x_ref, onorm_ref, win_ref, convw_ref, convb_ref, wout_ref,
                   cnorm_ref, wq_ref, wo_ref, mkt_ref, mv_ref, fnorm_ref,
                   o_ref, h_ref, z_ref):
    t = pl.program_id(1)

    @pl.when(t == 0)
    def _():
        z_ref[0:C_PAD, :] = jnp.zeros((C_PAD, D_MODEL), F32)

    @pl.when(t > 0)
    def _():
        z_ref[0:C_PAD, :] = z_ref[TS:TS + C_PAD, :]

    blocks = [slice(s0, s0 + SUB1) for s0 in range(0, TS, SUB1)]
    xs = [x_ref[0, rows, :] for rows in blocks]
    for rows, x in zip(blocks, xs):
        xn = _rmsnorm(x, onorm_ref[...]).astype(BF16)
        h_ref[rows, :] = _dot(xn, win_ref[...])
    x1s = []
    for rows, x in zip(blocks, xs):
        zrows = lambda back: slice(C_PAD - back + rows.start, C_PAD - back + rows.stop)
        z = h_ref[rows, D_MODEL:2 * D_MODEL] * h_ref[rows, 2 * D_MODEL:3 * D_MODEL]
        z_ref[zrows(0), :] = z
        conv = convb_ref[...] + z * convw_ref[2:3, :]
        conv = conv + z_ref[zrows(1), :] * convw_ref[1:2, :]
        conv = conv + z_ref[zrows(2), :] * convw_ref[0:1, :]
        y = h_ref[rows, 0:D_MODEL] * conv * _silu(h_ref[rows, 3 * D_MODEL:4 * D_MODEL])
        x1s.append(x + _dot(y.astype(BF16), wout_ref[...]))
    x2s = _cross_attention_blocks(x1s, cnorm_ref, wq_ref, wo_ref, mkt_ref, mv_ref)
    for rows, x2 in zip(blocks, x2s):
        o_ref[0, rows, :] = _rmsnorm(x2, fnorm_ref[...])


def _const_spec(shape):
    return pl.BlockSpec(shape, lambda *_: (0,) * len(shape), pipeline_mode=pl.Buffered(1))


def _retention_tables(seq):
    f32 = np.float32
    half = B_DIM // 2
    inv = (1.0 / (f32(ROT_BASE) ** np.linspace(0.0, 1.0, half, dtype=f32))).astype(f32)
    ang = np.arange(seq, dtype=f32)[:, None] * inv[None, :]
    cos, sin = np.cos(ang), np.sin(ang)
    cos2 = np.concatenate([cos, cos], axis=1)
    sin2 = np.concatenate([-sin, sin], axis=1)
    log_gamma = np.log1p(-np.exp2(-5.0 - np.arange(B_HEADS, dtype=f32))).astype(f32)
    idx = np.arange(B_GROUP, dtype=f32)
    chunk_of = np.arange(B_GROUP) // CHUNK
    visible = (chunk_of[None, :] <= chunk_of[:, None]).astype(f32)
    scale = f32(B_DIM ** -0.5)
    dmat = np.exp(log_gamma[:, None, None] * np.abs(idx[:, None] - idx[None, :])) * visible * scale
    qdec = np.exp(log_gamma[:, None] * (idx + 1.0)[None, :])
    kdec = np.exp(log_gamma[:, None] * (B_GROUP - 1 - idx)[None, :]) * scale
    qdec = np.broadcast_to(qdec[:, :, None], (B_HEADS, B_GROUP, B_DIM))
    kdec = np.broadcast_to(kdec[:, :, None], (B_HEADS, B_GROUP, B_DIM))
    gdec = np.broadcast_to(np.exp(log_gamma * B_GROUP)[:, None, None], (B_HEADS, 1, B_DIM))
    return tuple(np.ascontiguousarray(a, dtype=f32) for a in (cos2, sin2, dmat, qdec, kdec, gdec))


def _cast_specs(n_steps, step_of, layers, widths):
    rows = D_MODEL // n_steps
    assert rows * n_steps == D_MODEL and rows % 16 == 0
    ins = [pl.BlockSpec((None, rows, w), lambda *g, l=l: (l, step_of(*g), 0))
           for w, l in zip(widths, layers)]
    outs = [pl.BlockSpec((rows, w), lambda *g: (step_of(*g), 0)) for w in widths]
    shapes = [jax.ShapeDtypeStruct((D_MODEL, w), BF16) for w in widths]
    return ins, outs, shapes


def _forward(x, mem, e_norm, e_w_in, e_sink, e_w_out, o_norm, o_w_in, o_conv_w, o_conv_b,
             o_w_out, c_norm, c_mem_norm, c_wq, c_wkv, c_wo, final_norm):
    batch, seq, d = x.shape
    depth = c_wq.shape[0]
    assert d == D_MODEL and depth == 2 and seq % TS == 0 and mem.shape == (batch, MEM_LEN, D_MODEL)
    n_t = seq // TS
    params = pltpu.CompilerParams(dimension_semantics=("arbitrary", "arbitrary"),
                                  vmem_limit_bytes=VMEM_LIMIT_BYTES)

    c_in, c_out, c_shape = _cast_specs(depth * batch, lambda i, b: i * batch + b, (0, 0, 0, 0),
                                       (EVEN_IN, D_MODEL, D_MODEL, D_MODEL))
    mkt, mv, w_in0, w_out0, wq0, wo0 = pl.pallas_call(
        _memkv_kernel,
        grid=(depth, batch),
        in_specs=[pl.BlockSpec((1, MEM_LEN, D_MODEL), lambda i, b: (b, 0, 0)),
                  pl.BlockSpec((1, 1, D_MODEL), lambda i, b: (i, 0, 0)),
                  pl.BlockSpec((None, D_MODEL, 2 * D_MODEL), lambda i, b: (i, 0, 0))] + c_in,
        out_specs=[pl.BlockSpec((1, 1, D_MODEL, MEM_LEN), lambda i, b: (i, b, 0, 0)),
                   pl.BlockSpec((1, 1, MEM_LEN, D_MODEL), lambda i, b: (i, b, 0, 0))] + c_out,
        out_shape=[jax.ShapeDtypeStruct((depth, batch, D_MODEL, MEM_LEN), BF16),
                   jax.ShapeDtypeStruct((depth, batch, MEM_LEN, D_MODEL), BF16)] + c_shape,
        scratch_shapes=[pltpu.VMEM((D_MODEL, 2 * D_MODEL), BF16)],
        compiler_params=params,
        name="memkv",
    )(mem, c_mem_norm.reshape(depth, 1, D_MODEL), c_wkv, e_w_in, e_w_out, c_wq, c_wo)

    cos2, sin2, dmat, qdec, kdec, gdec = _retention_tables(seq)
    sink_rows = jnp.broadcast_to((e_sink[0].astype(F32) * LOG2E)[:, None, None],
                                 (A_Q_HEADS, CHUNK, A_KV_WIDTH)).reshape(A_Q_HEADS * CHUNK, A_KV_WIDTH)
    sink_pad = jnp.where(np.arange(A_KV_WIDTH)[None, :] == A_BAND - A_KV_WIDTH,
                         sink_rows, NEG_INF).astype(F32)

    tile_spec = pl.BlockSpec((1, TS, D_MODEL), lambda b, t: (b, t, 0))
    mkt_spec = lambda i: pl.BlockSpec((None, 1, D_MODEL, MEM_LEN), lambda b, t: (i, b, 0, 0))
    mv_spec = lambda i: pl.BlockSpec((None, 1, MEM_LEN, D_MODEL), lambda b, t: (i, b, 0, 0))
    c_in, c_out, c_shape = _cast_specs(batch * n_t, lambda b, t: b * n_t + t, (0, 0, 1, 1),
                                       (4 * D_MODEL, D_MODEL, D_MODEL, D_MODEL))

    x1, w_in1, w_out1, wq1, wo1 = pl.pallas_call(
        _layer0_kernel,
        grid=(batch, n_t),
        in_specs=[tile_spec,
                  pl.BlockSpec((TS, B_DIM), lambda b, t: (t, 0)),
                  pl.BlockSpec((TS, B_DIM), lambda b, t: (t, 0)),
                  _const_spec((B_HEADS, B_GROUP, B_GROUP)),
                  _const_spec((B_HEADS, B_GROUP, B_DIM)),
                  _const_spec((B_HEADS, B_GROUP, B_DIM)),
                  _const_spec((B_HEADS, 1, B_DIM)),
                  _const_spec((A_Q_HEADS * CHUNK, A_KV_WIDTH)),
                  _const_spec((1, D_MODEL)),
                  _const_spec((D_MODEL, EVEN_IN)),
                  _const_spec((MIX_WIDTH, D_MODEL)),
                  _const_spec((1, D_MODEL)),
                  _const_spec((D_MODEL, D_MODEL)),
                  _const_spec((D_MODEL, D_MODEL)),
                  mkt_spec(0), mv_spec(0)] + c_in,
        out_specs=[tile_spec] + c_out,
        out_shape=[jax.ShapeDtypeStruct((batch, seq, D_MODEL), F32)] + c_shape,
        scratch_shapes=[pltpu.VMEM((TS, EVEN_IN), F32),
                        pltpu.VMEM((TS, MIX_WIDTH), BF16),
                        pltpu.VMEM((A_KV_HEADS, A_TAIL + TS, A_KV_WIDTH), BF16),
                        pltpu.VMEM((A_KV_HEADS, A_TAIL + TS, 2 * A_KV_WIDTH), BF16),
                        pltpu.VMEM((B_HEADS, B_DIM, B_DIM), F32)],
        compiler_params=params,
        name="layer0",
    )(x, cos2, sin2, dmat, qdec, kdec, gdec, sink_pad,
      e_norm[0].reshape(1, D_MODEL), w_in0, w_out0,
      c_norm[0].reshape(1, D_MODEL), wq0, wo0, mkt, mv,
      o_w_in, o_w_out, c_wq, c_wo)

    out = pl.pallas_call(
        _layer1_kernel,
        grid=(batch, n_t),
        in_specs=[tile_spec,
                  _const_spec((1, D_MODEL)),
                  _const_spec((D_MODEL, 4 * D_MODEL)),
                  _const_spec((C_CONV, D_MODEL)),
                  _const_spec((1, D_MODEL)),
                  _const_spec((D_MODEL, D_MODEL)),
                  _const_spec((1, D_MODEL)),
                  _const_spec((D_MODEL, D_MODEL)),
                  _const_spec((D_MODEL, D_MODEL)),
                  mkt_spec(1), mv_spec(1),
                  _const_spec((1, D_MODEL))],
        out_specs=tile_spec,
        out_shape=jax.ShapeDtypeStruct((batch, seq, D_MODEL), F32),
        scratch_shapes=[pltpu.VMEM((TS, 4 * D_MODEL), F32),
                        pltpu.VMEM((C_PAD + TS, D_MODEL), F32)],
        compiler_params=params,
        name="layer1",
    )(x1, o_norm[0].reshape(1, D_MODEL), w_in1, o_conv_w[0],
      o_conv_b[0].reshape(1, D_MODEL), w_out1,
      c_norm[1].reshape(1, D_MODEL), wq1, wo1, mkt, mv, final_norm.reshape(1, D_MODEL))
    return x1, out


def kernel(x, mem, e_norm, e_w_in, e_sink, e_w_out, o_norm, o_w_in, o_conv_w, o_conv_b,
           o_w_out, c_norm, c_mem_norm, c_wq, c_wkv, c_wo, final_norm):
    return _forward(x, mem, e_norm, e_w_in, e_sink, e_w_out, o_norm, o_w_in, o_conv_w, o_conv_b,
                    o_w_out, c_norm, c_mem_norm, c_wq, c_wkv, c_wo, final_norm)[1]
```

```python
import jax
import jax.numpy as jnp
import numpy as np
from jax import lax
from jax.experimental import pallas as pl
from jax.experimental.pallas import tpu as pltpu

F32 = jnp.float32
BF16 = jnp.bfloat16

D_MODEL = 1024
CHUNK = 64
MEM_LEN = 256
EPS = 1e-6
NEG_INF = -1e30
LOG2E = 1.4426950408889634

A_Q_HEADS = 8
A_KV_HEADS = 2
A_HEAD_DIM = 64
A_WIN_CHUNKS = 2
A_WIDTH = A_Q_HEADS * A_HEAD_DIM
A_KV_WIDTH = A_KV_HEADS * A_HEAD_DIM
A_BAND = (A_WIN_CHUNKS + 1) * CHUNK
A_TAIL = A_WIN_CHUNKS * CHUNK
A_SCALE = A_HEAD_DIM ** -0.5 * LOG2E

B_HEADS = 4
B_DIM = 128
B_WIDTH = B_HEADS * B_DIM
ROT_BASE = 10000.0
B_GROUP = 256

C_CONV = 3
C_PAD = 8

X_HEADS = 4
X_DIM = D_MODEL // X_HEADS
X_SCALE = X_DIM ** -0.5 * LOG2E

OFF_AQ = 0
OFF_AK = OFF_AQ + A_WIDTH
OFF_AV = OFF_AK + A_KV_WIDTH
OFF_AG = OFF_AV + A_KV_WIDTH
OFF_BQ = OFF_AG + A_WIDTH
OFF_BK = OFF_BQ + B_WIDTH
OFF_BV = OFF_BK + B_WIDTH
OFF_BG = OFF_BV + B_WIDTH
EVEN_IN = OFF_BG + B_WIDTH
MIX_WIDTH = A_WIDTH + B_WIDTH

TS = 1024
SUB0 = 256
SUB1 = 256
VMEM_LIMIT_BYTES = 56 * 1024 * 1024


def _rmsnorm(x, g):
    return x * lax.rsqrt(jnp.mean(x * x, axis=-1, keepdims=True) + EPS) * g


def _silu(x):
    return x * jax.nn.sigmoid(x)


def _dot(a, b):
    return jnp.dot(a, b, preferred_element_type=F32)


def _dot_nt(a, b):
    return lax.dot_general(a, b, (((1,), (1,)), ((), ())), preferred_element_type=F32)


def _cast_blocks(pairs):
    for src_ref, dst_ref in pairs:
        dst_ref[...] = src_ref[...].astype(BF16)


def _memkv_kernel(mem_ref, g_ref, wkv_ref, wq_ref, wo_ref, win_ref, wout_ref,
                  mq_ref, mo_ref, win_bf_ref, wout_bf_ref, wkv_bf_ref, wq_bf_ref, wo_bf_ref):
    @pl.when(pl.program_id(1) == 0)
    def _():
        wkv_bf_ref[...] = wkv_ref[...].astype(BF16)
        wq_bf_ref[...] = wq_ref[...].astype(BF16)
        wo_bf_ref[...] = wo_ref[...].astype(BF16)

    mn = _rmsnorm(mem_ref[0], g_ref[0]).astype(BF16)
    kv = _dot(mn, wkv_bf_ref[...])
    k = (kv[:, :D_MODEL] * X_SCALE).astype(BF16)
    v = kv[:, D_MODEL:].astype(BF16)
    for h in range(X_HEADS):
        cs = slice(h * X_DIM, (h + 1) * X_DIM)
        mq_ref[0, 0, :, h * MEM_LEN:(h + 1) * MEM_LEN] = _dot_nt(wq_bf_ref[:, cs], k[:, cs]).astype(BF16)
        mo_ref[0, 0, h * MEM_LEN:(h + 1) * MEM_LEN, :] = _dot(v[:, cs], wo_bf_ref[cs, :]).astype(BF16)
    _cast_blocks([(win_ref, win_bf_ref), (wout_ref, wout_bf_ref)])


def _cross_attention_blocks(x1s, cnorm_ref, mq_ref, mo_ref):
    scores = [_dot(_rmsnorm(x1, cnorm_ref[...]).astype(BF16), mq_ref[0]) for x1 in x1s]
    probs = []
    for sc in scores:
        heads = []
        for h in range(X_HEADS):
            s = sc[:, h * MEM_LEN:(h + 1) * MEM_LEN]
            p = jnp.exp2(s - jnp.max(s, axis=-1, keepdims=True))
            heads.append((p * (1.0 / jnp.sum(p, axis=-1, keepdims=True))).astype(BF16))
        probs.append(jnp.concatenate(heads, axis=1))
    return [x1 + _dot(p, mo_ref[0]) for x1, p in zip(x1s, probs)]


def _layer0_kernel(x_ref, cos_ref, sin_ref, dmat_ref, qdec_ref, kdec_ref, gdec_ref, sink_ref,
                   enorm_ref, win_ref, wout_ref, cnorm_ref, mq_ref, mo_ref,
                   nwin_ref, nwout_ref,
                   o_ref, nwin_bf_ref, nwout_bf_ref,
                   h_ref, y_ref, k_ref, v_ref, state_ref):
    t = pl.program_id(1)
    _cast_blocks([(nwin_ref, nwin_bf_ref), (nwout_ref, nwout_bf_ref)])

    @pl.when(t == 0)
    def _():
        k_ref[:, 0:A_TAIL, :] = jnp.zeros((A_KV_HEADS, A_TAIL, A_KV_WIDTH), BF16)
        v_ref[:, 0:A_TAIL, 0:A_KV_WIDTH] = jnp.zeros((A_KV_HEADS, A_TAIL, A_KV_WIDTH), BF16)
        v_ref[:, :, A_KV_WIDTH:] = jnp.ones((A_KV_HEADS, A_TAIL + TS, A_KV_WIDTH), BF16)
        state_ref[...] = jnp.zeros_like(state_ref)

    @pl.when(t > 0)
    def _():
        k_ref[:, 0:A_TAIL, :] = k_ref[:, TS:TS + A_TAIL, :]
        v_ref[:, 0:A_TAIL, 0:A_KV_WIDTH] = v_ref[:, TS:TS + A_TAIL, 0:A_KV_WIDTH]

    lo = lax.broadcasted_iota(jnp.int32, (CHUNK, A_KV_WIDTH), 1) < A_HEAD_DIM
    lo_blk = lax.broadcasted_iota(jnp.int32, (SUB0, A_KV_WIDTH), 1) < A_HEAD_DIM
    kidx = lax.broadcasted_iota(jnp.int32, (4 * CHUNK, A_KV_WIDTH), 1)
    hi_half = kidx >= A_BAND - A_KV_WIDTH
    pad_k = jnp.zeros((2 * A_KV_WIDTH - A_BAND, A_KV_WIDTH), BF16)
    pad_r = lax.broadcasted_iota(jnp.int32, (2 * A_KV_WIDTH - A_BAND, 2 * A_KV_WIDTH), 0)
    pad_c = lax.broadcasted_iota(jnp.int32, (2 * A_KV_WIDTH - A_BAND, 2 * A_KV_WIDTH), 1)
    pad_v = jnp.where((pad_r == 0) & (pad_c >= A_KV_WIDTH), 1.0, 0.0).astype(BF16)

    def store_kv(rows):
        hist = slice(A_TAIL + rows.start, A_TAIL + rows.stop)
        kk = h_ref[rows, OFF_AK:OFF_AK + A_KV_WIDTH] * A_SCALE
        vv = h_ref[rows, OFF_AV:OFF_AV + A_KV_WIDTH]
        kr = pltpu.roll(kk, A_HEAD_DIM, 1)
        vr = pltpu.roll(vv, A_HEAD_DIM, 1)
        k_ref[0, hist, :] = jnp.where(lo_blk, kk, kr).astype(BF16)
        k_ref[1, hist, :] = jnp.where(lo_blk, kr, kk).astype(BF16)
        v_ref[0, hist, 0:A_KV_WIDTH] = jnp.where(lo_blk, vv, vr).astype(BF16)
        v_ref[1, hist, 0:A_KV_WIDTH] = jnp.where(lo_blk, vr, vv).astype(BF16)

    def swa_chunk(c):
        r0 = c * CHUNK
        band = slice(r0, r0 + A_BAND)
        q = h_ref[r0:r0 + CHUNK, OFF_AQ:OFF_AQ + A_WIDTH]
        cols = []
        for g in range(A_KV_HEADS):
            qa, qb = q[:, (2 * g) * 128:(2 * g + 1) * 128], q[:, (2 * g + 1) * 128:(2 * g + 2) * 128]
            zero = jnp.zeros_like(qa)
            lhs = jnp.concatenate([jnp.where(lo, qa, zero), jnp.where(lo, zero, qa),
                                   jnp.where(lo, qb, zero), jnp.where(lo, zero, qb)],
                                  axis=0).astype(BF16)
            s = _dot_nt(lhs, jnp.concatenate([k_ref[g, band, :], pad_k], axis=0))
            s0 = s[:, :A_KV_WIDTH]
            s1 = jnp.where(hi_half, sink_ref[g * 4 * CHUNK:(g + 1) * 4 * CHUNK, :],
                           s[:, A_KV_WIDTH:])
            if c < A_WIN_CHUNKS:
                n_masked = jnp.where(t == 0, (A_WIN_CHUNKS - c) * CHUNK, 0)
                s0 = jnp.where(kidx < n_masked, NEG_INF, s0)
            m = jnp.max(jnp.maximum(s0, s1), axis=-1, keepdims=True)
            p = jnp.concatenate([jnp.exp2(s0 - m), jnp.exp2(s1 - m)], axis=1).astype(BF16)
            pv = _dot(p, jnp.concatenate([v_ref[g, band, :], pad_v], axis=0))
            o = pv[:, :A_KV_WIDTH] / pv[:, A_KV_WIDTH:]
            cols += [jnp.where(lo, o[0:CHUNK], o[CHUNK:2 * CHUNK]),
                     jnp.where(lo, o[2 * CHUNK:3 * CHUNK], o[3 * CHUNK:4 * CHUNK])]
        ya = jnp.concatenate(cols, axis=1)
        ag = h_ref[r0:r0 + CHUNK, OFF_AG:OFF_AG + A_WIDTH]
        y_ref[r0:r0 + CHUNK, 0:A_WIDTH] = (ya * _silu(ag)).astype(BF16)

    def retention_group(rows):
        cos = cos_ref[rows, :]
        sin = sin_ref[rows, :]
        for h in range(B_HEADS):
            q = h_ref[rows, OFF_BQ + h * B_DIM:OFF_BQ + (h + 1) * B_DIM]
            k = h_ref[rows, OFF_BK + h * B_DIM:OFF_BK + (h + 1) * B_DIM]
            v = h_ref[rows, OFF_BV + h * B_DIM:OFF_BV + (h + 1) * B_DIM].astype(BF16)
            qr = q * cos + pltpu.roll(q, B_DIM // 2, 1) * sin
            kr = k * cos + pltpu.roll(k, B_DIM // 2, 1) * sin
            sc = _dot_nt(qr.astype(BF16), kr.astype(BF16)) * dmat_ref[h]
            state = state_ref[h]
            lhs = jnp.concatenate([sc.astype(BF16), (qr * qdec_ref[h]).astype(BF16)], axis=1)
            rhs = jnp.concatenate([v, state.astype(BF16)], axis=0)
            o = _dot(lhs, rhs)
            o = o * lax.rsqrt(jnp.mean(o * o, axis=-1, keepdims=True) + EPS)
            kd = (kr * kdec_ref[h]).T.astype(BF16)
            state_ref[h] = state * gdec_ref[h] + _dot(kd, v)
            bg = h_ref[rows, OFF_BG + h * B_DIM:OFF_BG + (h + 1) * B_DIM]
            y_ref[rows, A_WIDTH + h * B_DIM:A_WIDTH + (h + 1) * B_DIM] = (o * _silu(bg)).astype(BF16)

    blocks = [slice(s0, s0 + SUB0) for s0 in range(0, TS, SUB0)]
    xs = [x_ref[0, rows, :] for rows in blocks]
    for rows, x in zip(blocks, xs):
        xn = _rmsnorm(x, enorm_ref[...]).astype(BF16)
        h_ref[rows, :] = _dot(xn, win_ref[...])
        store_kv(rows)
    x1s = []
    for rows, x in zip(blocks, xs):
        for c in range(rows.start // CHUNK, rows.stop // CHUNK):
            swa_chunk(c)
        for g0 in range(rows.start, rows.stop, B_GROUP):
            retention_group(slice(g0, g0 + B_GROUP))
        x1s.append(x + _dot(y_ref[rows, :], wout_ref[...]))
    x2s = _cross_attention_blocks(x1s, cnorm_ref, mq_ref, mo_ref)
    for rows, x2 in zip(blocks, x2s):
        o_ref[0, rows, :] = x2


def _layer1_kernel(x_ref, onorm_ref, win_ref, convw_ref, convb_ref, wout_ref,
                   cnorm_ref, mq_ref, mo_ref, fnorm_ref,
                   o_ref, h_ref, z_ref):
    t = pl.program_id(1)

    @pl.when(t == 0)
    def _():
        z_ref[0:C_PAD, :] = jnp.zeros((C_PAD, D_MODEL), F32)

    @pl.when(t > 0)
    def _():
        z_ref[0:C_PAD, :] = z_ref[TS:TS + C_PAD, :]

    blocks = [slice(s0, s0 + SUB1) for s0 in range(0, TS, SUB1)]
    xs = [x_ref[0, rows, :] for rows in blocks]
    for rows, x in zip(blocks, xs):
        xn = _rmsnorm(x, onorm_ref[...]).astype(BF16)
        h_ref[rows, :] = _dot(xn, win_ref[...])
    x1s = []
    for rows, x in zip(blocks, xs):
        zrows = lambda back: slice(C_PAD - back + rows.start, C_PAD - back + rows.stop)
        z = h_ref[rows, D_MODEL:2 * D_MODEL] * h_ref[rows, 2 * D_MODEL:3 * D_MODEL]
        z_ref[zrows(0), :] = z
        conv = convb_ref[...] + z * convw_ref[2:3, :]
        conv = conv + z_ref[zrows(1), :] * convw_ref[1:2, :]
        conv = conv + z_ref[zrows(2), :] * convw_ref[0:1, :]
        y = h_ref[rows, 0:D_MODEL] * conv * _silu(h_ref[rows, 3 * D_MODEL:4 * D_MODEL])
        x1s.append(x + _dot(y.astype(BF16), wout_ref[...]))
    x2s = _cross_attention_blocks(x1s, cnorm_ref, mq_ref, mo_ref)
    for rows, x2 in zip(blocks, x2s):
        o_ref[0, rows, :] = _rmsnorm(x2, fnorm_ref[...])


def _const_spec(shape):
    return pl.BlockSpec(shape, lambda *_: (0,) * len(shape), pipeline_mode=pl.Buffered(1))


def _retention_tables(seq):
    f32 = np.float32
    half = B_DIM // 2
    inv = (1.0 / (f32(ROT_BASE) ** np.linspace(0.0, 1.0, half, dtype=f32))).astype(f32)
    ang = np.arange(seq, dtype=f32)[:, None] * inv[None, :]
    cos, sin = np.cos(ang), np.sin(ang)
    cos2 = np.concatenate([cos, cos], axis=1)
    sin2 = np.concatenate([-sin, sin], axis=1)
    log_gamma = np.log1p(-np.exp2(-5.0 - np.arange(B_HEADS, dtype=f32))).astype(f32)
    idx = np.arange(B_GROUP, dtype=f32)
    chunk_of = np.arange(B_GROUP) // CHUNK
    visible = (chunk_of[None, :] <= chunk_of[:, None]).astype(f32)
    scale = f32(B_DIM ** -0.5)
    dmat = np.exp(log_gamma[:, None, None] * np.abs(idx[:, None] - idx[None, :])) * visible * scale
    qdec = np.exp(log_gamma[:, None] * (idx + 1.0)[None, :])
    kdec = np.exp(log_gamma[:, None] * (B_GROUP - 1 - idx)[None, :]) * scale
    qdec = np.broadcast_to(qdec[:, :, None], (B_HEADS, B_GROUP, B_DIM))
    kdec = np.broadcast_to(kdec[:, :, None], (B_HEADS, B_GROUP, B_DIM))
    gdec = np.broadcast_to(np.exp(log_gamma * B_GROUP)[:, None, None], (B_HEADS, 1, B_DIM))
    return tuple(np.ascontiguousarray(a, dtype=f32) for a in (cos2, sin2, dmat, qdec, kdec, gdec))


def _cast_specs(n_steps, step_of, layers, widths):
    rows = D_MODEL // n_steps
    assert rows * n_steps == D_MODEL and rows % 16 == 0
    ins = [pl.BlockSpec((None, rows, w), lambda *g, l=l: (l, step_of(*g), 0))
           for w, l in zip(widths, layers)]
    outs = [pl.BlockSpec((rows, w), lambda *g: (step_of(*g), 0)) for w in widths]
    shapes = [jax.ShapeDtypeStruct((D_MODEL, w), BF16) for w in widths]
    return ins, outs, shapes


def _forward(x, mem, e_norm, e_w_in, e_sink, e_w_out, o_norm, o_w_in, o_conv_w, o_conv_b,
             o_w_out, c_norm, c_mem_norm, c_wq, c_wkv, c_wo, final_norm):
    batch, seq, d = x.shape
    depth = c_wq.shape[0]
    assert d == D_MODEL and depth == 2 and seq % TS == 0 and mem.shape == (batch, MEM_LEN, D_MODEL)
    n_t = seq // TS
    params = pltpu.CompilerParams(dimension_semantics=("arbitrary", "arbitrary"),
                                  vmem_limit_bytes=VMEM_LIMIT_BYTES)

    xw = X_HEADS * MEM_LEN
    layer_spec = lambda shape, **kw: pl.BlockSpec((None,) + shape, lambda i, b: (i, 0, 0), **kw)
    single = dict(pipeline_mode=pl.Buffered(1))
    c_in, c_out, c_shape = _cast_specs(depth * batch, lambda i, b: i * batch + b, (0, 0),
                                       (EVEN_IN, D_MODEL))
    mq, mo, w_in0, w_out0 = pl.pallas_call(
        _memkv_kernel,
        grid=(depth, batch),
        in_specs=[pl.BlockSpec((1, MEM_LEN, D_MODEL), lambda i, b: (b, 0, 0)),
                  pl.BlockSpec((1, 1, D_MODEL), lambda i, b: (i, 0, 0)),
                  layer_spec((D_MODEL, 2 * D_MODEL)),
                  layer_spec((D_MODEL, D_MODEL), **single),
                  layer_spec((D_MODEL, D_MODEL), **single)] + c_in,
        out_specs=[pl.BlockSpec((1, 1, D_MODEL, xw), lambda i, b: (i, b, 0, 0)),
                   pl.BlockSpec((1, 1, xw, D_MODEL), lambda i, b: (i, b, 0, 0))] + c_out,
        out_shape=[jax.ShapeDtypeStruct((depth, batch, D_MODEL, xw), BF16),
                   jax.ShapeDtypeStruct((depth, batch, xw, D_MODEL), BF16)] + c_shape,
        scratch_shapes=[pltpu.VMEM((D_MODEL, 2 * D_MODEL), BF16),
                        pltpu.VMEM((D_MODEL, D_MODEL), BF16),
                        pltpu.VMEM((D_MODEL, D_MODEL), BF16)],
        compiler_params=params,
        name="memkv",
    )(mem, c_mem_norm.reshape(depth, 1, D_MODEL), c_wkv, c_wq, c_wo, e_w_in, e_w_out)

    cos2, sin2, dmat, qdec, kdec, gdec = _retention_tables(seq)
    sink_rows = jnp.broadcast_to((e_sink[0].astype(F32) * LOG2E)[:, None, None],
                                 (A_Q_HEADS, CHUNK, A_KV_WIDTH)).reshape(A_Q_HEADS * CHUNK, A_KV_WIDTH)
    sink_pad = jnp.where(np.arange(A_KV_WIDTH)[None, :] == A_BAND - A_KV_WIDTH,
                         sink_rows, NEG_INF).astype(F32)

    tile_spec = pl.BlockSpec((1, TS, D_MODEL), lambda b, t: (b, t, 0))
    mq_spec = lambda i: pl.BlockSpec((None, 1, D_MODEL, xw), lambda b, t: (i, b, 0, 0),
                                     pipeline_mode=pl.Buffered(1))
    mo_spec = lambda i: pl.BlockSpec((None, 1, xw, D_MODEL), lambda b, t: (i, b, 0, 0),
                                     pipeline_mode=pl.Buffered(1))
    c_in, c_out, c_shape = _cast_specs(batch * n_t, lambda b, t: b * n_t + t, (0, 0),
                                       (4 * D_MODEL, D_MODEL))

    x1, w_in1, w_out1 = pl.pallas_call(
        _layer0_kernel,
        grid=(batch, n_t),
        in_specs=[tile_spec,
                  pl.BlockSpec((TS, B_DIM), lambda b, t: (t, 0)),
                  pl.BlockSpec((TS, B_DIM), lambda b, t: (t, 0)),
                  _const_spec((B_HEADS, B_GROUP, B_GROUP)),
                  _const_spec((B_HEADS, B_GROUP, B_DIM)),
                  _const_spec((B_HEADS, B_GROUP, B_DIM)),
                  _const_spec((B_HEADS, 1, B_DIM)),
                  _const_spec((A_Q_HEADS * CHUNK, A_KV_WIDTH)),
                  _const_spec((1, D_MODEL)),
                  _const_spec((D_MODEL, EVEN_IN)),
                  _const_spec((MIX_WIDTH, D_MODEL)),
                  _const_spec((1, D_MODEL)),
                  mq_spec(0), mo_spec(0)] + c_in,
        out_specs=[tile_spec] + c_out,
        out_shape=[jax.ShapeDtypeStruct((batch, seq, D_MODEL), F32)] + c_shape,
        scratch_shapes=[pltpu.VMEM((TS, EVEN_IN), F32),
                        pltpu.VMEM((TS, MIX_WIDTH), BF16),
                        pltpu.VMEM((A_KV_HEADS, A_TAIL + TS, A_KV_WIDTH), BF16),
                        pltpu.VMEM((A_KV_HEADS, A_TAIL + TS, 2 * A_KV_WIDTH), BF16),
                        pltpu.VMEM((B_HEADS, B_DIM, B_DIM), F32)],
        compiler_params=params,
        name="layer0",
    )(x, cos2, sin2, dmat, qdec, kdec, gdec, sink_pad,
      e_norm[0].reshape(1, D_MODEL), w_in0, w_out0,
      c_norm[0].reshape(1, D_MODEL), mq, mo,
      o_w_in, o_w_out)

    out = pl.pallas_call(
        _layer1_kernel,
        grid=(batch, n_t),
        in_specs=[tile_spec,
                  _const_spec((1, D_MODEL)),
                  _const_spec((D_MODEL, 4 * D_MODEL)),
                  _const_spec((C_CONV, D_MODEL)),
                  _const_spec((1, D_MODEL)),
                  _const_spec((D_MODEL, D_MODEL)),
                  _const_spec((1, D_MODEL)),
                  mq_spec(1), mo_spec(1),
                  _const_spec((1, D_MODEL))],
        out_specs=tile_spec,
        out_shape=jax.ShapeDtypeStruct((batch, seq, D_MODEL), F32),
        scratch_shapes=[pltpu.VMEM((TS, 4 * D_MODEL), F32),
                        pltpu.VMEM((C_PAD + TS, D_MODEL), F32)],
        compiler_params=params,
        name="layer1",
    )(x1, o_norm[0].reshape(1, D_MODEL), w_in1, o_conv_w[0],
      o_conv_b[0].reshape(1, D_MODEL), w_out1,
      c_norm[1].reshape(1, D_MODEL), mq, mo, final_norm.reshape(1, D_MODEL))
    return x1, out


def kernel(x, mem, e_norm, e_w_in, e_sink, e_w_out, o_norm, o_w_in, o_conv_w, o_conv_b,
           o_w_out, c_norm, c_mem_norm, c_wq, c_wkv, c_wo, final_norm):
    return _forward(x, mem, e_norm, e_w_in, e_sink, e_w_out, o_norm, o_w_in, o_conv_w, o_conv_b,
                    o_w_out, c_norm, c_mem_norm, c_wq, c_wkv, c_wo, final_norm)[1]
```

```python
import jax
import jax.numpy as jnp
import numpy as np
from jax import lax
from jax.experimental import pallas as pl
from jax.experimental.pallas import tpu as pltpu

F32 = jnp.float32
BF16 = jnp.bfloat16

D_MODEL = 1024
CHUNK = 64
MEM_LEN = 256
EPS = 1e-6
NEG_INF = -1e30
LOG2E = 1.4426950408889634

A_Q_HEADS = 8
A_KV_HEADS = 2
A_HEAD_DIM = 64
A_WIN_CHUNKS = 2
A_WIDTH = A_Q_HEADS * A_HEAD_DIM
A_KV_WIDTH = A_KV_HEADS * A_HEAD_DIM
A_BAND = (A_WIN_CHUNKS + 1) * CHUNK
A_TAIL = A_WIN_CHUNKS * CHUNK
A_SCALE = A_HEAD_DIM ** -0.5 * LOG2E

B_HEADS = 4
B_DIM = 128
B_WIDTH = B_HEADS * B_DIM
ROT_BASE = 10000.0
B_GROUP = 256

C_CONV = 3
C_PAD = 8

X_HEADS = 4
X_DIM = D_MODEL // X_HEADS
X_SCALE = X_DIM ** -0.5 * LOG2E

OFF_AQ = 0
OFF_AK = OFF_AQ + A_WIDTH
OFF_AV = OFF_AK + A_KV_WIDTH
OFF_AG = OFF_AV + A_KV_WIDTH
OFF_BQ = OFF_AG + A_WIDTH
OFF_BK = OFF_BQ + B_WIDTH
OFF_BV = OFF_BK + B_WIDTH
OFF_BG = OFF_BV + B_WIDTH
EVEN_IN = OFF_BG + B_WIDTH
MIX_WIDTH = A_WIDTH + B_WIDTH

TS = 1024
SUB0 = 256
SUB1 = 256
VMEM_LIMIT_BYTES = 56 * 1024 * 1024


def _rmsnorm(x, g):
    return x * lax.rsqrt(jnp.mean(x * x, axis=-1, keepdims=True) + EPS) * g


def _silu(x):
    return x * jax.nn.sigmoid(x)


def _dot(a, b):
    return jnp.dot(a, b, preferred_element_type=F32)


def _dot_nt(a, b):
    return lax.dot_general(a, b, (((1,), (1,)), ((), ())), preferred_element_type=F32)


def _cast_blocks(pairs):
    for src_ref, dst_ref in pairs:
        dst_ref[...] = src_ref[...].astype(BF16)


def _memkv_kernel(mem_ref, g_ref, wkv_ref, wq_ref, wo_ref, win_ref, wout_ref,
                  mq_ref, mo_ref, win_bf_ref, wout_bf_ref, wkv_bf_ref, wq_bf_ref, wo_bf_ref):
    @pl.when(pl.program_id(1) == 0)
    def _():
        wkv_bf_ref[...] = wkv_ref[...].astype(BF16)
        wq_bf_ref[...] = wq_ref[...].astype(BF16)
        wo_bf_ref[...] = wo_ref[...].astype(BF16)

    mn = _rmsnorm(mem_ref[0], g_ref[0]).astype(BF16)
    kv = _dot(mn, wkv_bf_ref[...])
    k = (kv[:, :D_MODEL] * X_SCALE).astype(BF16)
    v = kv[:, D_MODEL:].astype(BF16)
    for h in range(X_HEADS):
        cs = slice(h * X_DIM, (h + 1) * X_DIM)
        mq_ref[0, 0, :, h * MEM_LEN:(h + 1) * MEM_LEN] = _dot_nt(wq_bf_ref[:, cs], k[:, cs]).astype(BF16)
        mo_ref[0, 0, h * MEM_LEN:(h + 1) * MEM_LEN, :] = _dot(v[:, cs], wo_bf_ref[cs, :]).astype(BF16)
    _cast_blocks([(win_ref, win_bf_ref), (wout_ref, wout_bf_ref)])


def _softmax_heads(sc):
    heads = []
    for h in range(X_HEADS):
        s = sc[:, h * MEM_LEN:(h + 1) * MEM_LEN]
        p = jnp.exp2(s - jnp.max(s, axis=-1, keepdims=True))
        heads.append((p * (1.0 / jnp.sum(p, axis=-1, keepdims=True))).astype(BF16))
    return jnp.concatenate(heads, axis=1)


def _cross_attention_blocks(x1s, cnorm_ref, mq_ref, mo_ref):
    scores = [_dot(_rmsnorm(x1, cnorm_ref[...]).astype(BF16), mq_ref[0]) for x1 in x1s]
    probs = [_softmax_heads(sc) for sc in scores]
    return [x1 + _dot(p, mo_ref[0]) for x1, p in zip(x1s, probs)]


def _layer0_kernel(x_ref, cos_ref, sin_ref, dmat_ref, qdec_ref, kdec_ref, gdec_ref, sink_ref,
                   enorm_ref, win_ref, wout_ref, cnorm_ref, mq_ref, mo_ref,
                   nwin_ref, nwout_ref,
                   o_ref, nwin_bf_ref, nwout_bf_ref,
                   h_ref, y_ref, k_ref, v_ref, state_ref):
    t = pl.program_id(1)
    _cast_blocks([(nwin_ref, nwin_bf_ref), (nwout_ref, nwout_bf_ref)])

    @pl.when(t == 0)
    def _():
        k_ref[:, 0:A_TAIL, :] = jnp.zeros((A_KV_HEADS, A_TAIL, A_KV_WIDTH), BF16)
        v_ref[:, 0:A_TAIL, 0:A_KV_WIDTH] = jnp.zeros((A_KV_HEADS, A_TAIL, A_KV_WIDTH), BF16)
        v_ref[:, :, A_KV_WIDTH:] = jnp.ones((A_KV_HEADS, A_TAIL + TS, A_KV_WIDTH), BF16)
        state_ref[...] = jnp.zeros_like(state_ref)

    @pl.when(t > 0)
    def _():
        k_ref[:, 0:A_TAIL, :] = k_ref[:, TS:TS + A_TAIL, :]
        v_ref[:, 0:A_TAIL, 0:A_KV_WIDTH] = v_ref[:, TS:TS + A_TAIL, 0:A_KV_WIDTH]

    lo = lax.broadcasted_iota(jnp.int32, (CHUNK, A_KV_WIDTH), 1) < A_HEAD_DIM
    lo_blk = lax.broadcasted_iota(jnp.int32, (SUB0, A_KV_WIDTH), 1) < A_HEAD_DIM
    kidx = lax.broadcasted_iota(jnp.int32, (4 * CHUNK, A_KV_WIDTH), 1)
    hi_half = kidx >= A_BAND - A_KV_WIDTH
    pad_k = jnp.zeros((2 * A_KV_WIDTH - A_BAND, A_KV_WIDTH), BF16)
    pad_r = lax.broadcasted_iota(jnp.int32, (2 * A_KV_WIDTH - A_BAND, 2 * A_KV_WIDTH), 0)
    pad_c = lax.broadcasted_iota(jnp.int32, (2 * A_KV_WIDTH - A_BAND, 2 * A_KV_WIDTH), 1)
    pad_v = jnp.where((pad_r == 0) & (pad_c >= A_KV_WIDTH), 1.0, 0.0).astype(BF16)

    def store_kv(rows):
        hist = slice(A_TAIL + rows.start, A_TAIL + rows.stop)
        kk = h_ref[rows, OFF_AK:OFF_AK + A_KV_WIDTH] * A_SCALE
        vv = h_ref[rows, OFF_AV:OFF_AV + A_KV_WIDTH]
        kr = pltpu.roll(kk, A_HEAD_DIM, 1)
        vr = pltpu.roll(vv, A_HEAD_DIM, 1)
        k_ref[0, hist, :] = jnp.where(lo_blk, kk, kr).astype(BF16)
        k_ref[1, hist, :] = jnp.where(lo_blk, kr, kk).astype(BF16)
        v_ref[0, hist, 0:A_KV_WIDTH] = jnp.where(lo_blk, vv, vr).astype(BF16)
        v_ref[1, hist, 0:A_KV_WIDTH] = jnp.where(lo_blk, vr, vv).astype(BF16)

    def swa_scores(c):
        r0 = c * CHUNK
        q = h_ref[r0:r0 + CHUNK, OFF_AQ:OFF_AQ + A_WIDTH]
        out = []
        for g in range(A_KV_HEADS):
            qa, qb = q[:, (2 * g) * 128:(2 * g + 1) * 128], q[:, (2 * g + 1) * 128:(2 * g + 2) * 128]
            zero = jnp.zeros_like(qa)
            lhs = jnp.concatenate([jnp.where(lo, qa, zero), jnp.where(lo, zero, qa),
                                   jnp.where(lo, qb, zero), jnp.where(lo, zero, qb)],
                                  axis=0).astype(BF16)
            out.append(_dot_nt(lhs, jnp.concatenate([k_ref[g, r0:r0 + A_BAND, :], pad_k], axis=0)))
        return out

    def swa_probs(c, scores):
        out = []
        for g, s in enumerate(scores):
            s0 = s[:, :A_KV_WIDTH]
            s1 = jnp.where(hi_half, sink_ref[g * 4 * CHUNK:(g + 1) * 4 * CHUNK, :],
                           s[:, A_KV_WIDTH:])
            if c < A_WIN_CHUNKS:
                n_masked = jnp.where(t == 0, (A_WIN_CHUNKS - c) * CHUNK, 0)
                s0 = jnp.where(kidx < n_masked, NEG_INF, s0)
            m = jnp.max(jnp.maximum(s0, s1), axis=-1, keepdims=True)
            out.append(jnp.concatenate([jnp.exp2(s0 - m), jnp.exp2(s1 - m)], axis=1).astype(BF16))
        return out

    def swa_values(c, probs):
        r0 = c * CHUNK
        return [_dot(p, jnp.concatenate([v_ref[g, r0:r0 + A_BAND, :], pad_v], axis=0))
                for g, p in enumerate(probs)]

    def swa_finish(c, pvs):
        r0 = c * CHUNK
        cols = []
        for pv in pvs:
            o = pv[:, :A_KV_WIDTH] / pv[:, A_KV_WIDTH:]
            cols += [jnp.where(lo, o[0:CHUNK], o[CHUNK:2 * CHUNK]),
                     jnp.where(lo, o[2 * CHUNK:3 * CHUNK], o[3 * CHUNK:4 * CHUNK])]
        ya = jnp.concatenate(cols, axis=1)
        ag = h_ref[r0:r0 + CHUNK, OFF_AG:OFF_AG + A_WIDTH]
        y_ref[r0:r0 + CHUNK, 0:A_WIDTH] = (ya * _silu(ag)).astype(BF16)

    def ret_scores(rows, h):
        cos = cos_ref[rows, :]
        sin = sin_ref[rows, :]
        q = h_ref[rows, OFF_BQ + h * B_DIM:OFF_BQ + (h + 1) * B_DIM]
        k = h_ref[rows, OFF_BK + h * B_DIM:OFF_BK + (h + 1) * B_DIM]
        qr = q * cos + pltpu.roll(q, B_DIM // 2, 1) * sin
        kr = k * cos + pltpu.roll(k, B_DIM // 2, 1) * sin
        return qr, kr, _dot_nt(qr.astype(BF16), kr.astype(BF16))

    def ret_output(rows, h, qr, sc):
        v = h_ref[rows, OFF_BV + h * B_DIM:OFF_BV + (h + 1) * B_DIM].astype(BF16)
        lhs = jnp.concatenate([(sc * dmat_ref[h]).astype(BF16), (qr * qdec_ref[h]).astype(BF16)], axis=1)
        rhs = jnp.concatenate([v, state_ref[h].astype(BF16)], axis=0)
        return _dot(lhs, rhs)

    def ret_state(rows, h, kr):
        v = h_ref[rows, OFF_BV + h * B_DIM:OFF_BV + (h + 1) * B_DIM].astype(BF16)
        kd = (kr * kdec_ref[h]).T.astype(BF16)
        state_ref[h] = state_ref[h] * gdec_ref[h] + _dot(kd, v)

    def ret_finish(rows, h, o):
        o = o * lax.rsqrt(jnp.mean(o * o, axis=-1, keepdims=True) + EPS)
        bg = h_ref[rows, OFF_BG + h * B_DIM:OFF_BG + (h + 1) * B_DIM]
        y_ref[rows, A_WIDTH + h * B_DIM:A_WIDTH + (h + 1) * B_DIM] = (o * _silu(bg)).astype(BF16)

    def mix_block(rows):
        chunks = range(rows.start // CHUNK, rows.stop // CHUNK)
        scores = [swa_scores(c) for c in chunks]
        probs = [swa_probs(c, s) for c, s in zip(chunks, scores)]
        pvs = [swa_values(c, p) for c, p in zip(chunks, probs)]
        for c, pv in zip(chunks, pvs):
            swa_finish(c, pv)
        for g0 in range(rows.start, rows.stop, B_GROUP):
            grp = slice(g0, g0 + B_GROUP)
            qks = [ret_scores(grp, h) for h in range(B_HEADS)]
            outs = [ret_output(grp, h, qr, sc) for h, (qr, _, sc) in enumerate(qks)]
            for h, (_, kr, _) in enumerate(qks):
                ret_state(grp, h, kr)
            for h, o in enumerate(outs):
                ret_finish(grp, h, o)

    blocks = [slice(s0, s0 + SUB0) for s0 in range(0, TS, SUB0)]
    xs = [x_ref[0, rows, :] for rows in blocks]
    for rows, x in zip(blocks, xs):
        xn = _rmsnorm(x, enorm_ref[...]).astype(BF16)
        h_ref[rows, :] = _dot(xn, win_ref[...])
        store_kv(rows)
    x1s = []
    for rows, x in zip(blocks, xs):
        mix_block(rows)
        x1s.append(x + _dot(y_ref[rows, :], wout_ref[...]))
    x2s = _cross_attention_blocks(x1s, cnorm_ref, mq_ref, mo_ref)
    for rows, x2 in zip(blocks, x2s):
        o_ref[0, rows, :] = x2


def _layer1_kernel(x_ref, onorm_ref, win_ref, convw_ref, convb_ref, wout_ref,
                   cnorm_ref, mq_ref, mo_ref, fnorm_ref,
                   o_ref, h_ref, z_ref):
    t = pl.program_id(1)

    @pl.when(t == 0)
    def _():
        z_ref[0:C_PAD, :] = jnp.zeros((C_PAD, D_MODEL), F32)

    @pl.when(t > 0)
    def _():
        z_ref[0:C_PAD, :] = z_ref[TS:TS + C_PAD, :]

    blocks = [slice(s0, s0 + SUB1) for s0 in range(0, TS, SUB1)]
    xs = [x_ref[0, rows, :] for rows in blocks]
    for rows, x in zip(blocks, xs):
        xn = _rmsnorm(x, onorm_ref[...]).astype(BF16)
        h_ref[rows, :] = _dot(xn, win_ref[...])
    x1s = []
    for rows, x in zip(blocks, xs):
        zrows = lambda back: slice(C_PAD - back + rows.start, C_PAD - back + rows.stop)
        z = h_ref[rows, D_MODEL:2 * D_MODEL] * h_ref[rows, 2 * D_MODEL:3 * D_MODEL]
        z_ref[zrows(0), :] = z
        conv = convb_ref[...] + z * convw_ref[2:3, :]
        conv = conv + z_ref[zrows(1), :] * convw_ref[1:2, :]
        conv = conv + z_ref[zrows(2), :] * convw_ref[0:1, :]
        y = h_ref[rows, 0:D_MODEL] * conv * _silu(h_ref[rows, 3 * D_MODEL:4 * D_MODEL])
        x1s.append(x + _dot(y.astype(BF16), wout_ref[...]))
    x2s = _cross_attention_blocks(x1s, cnorm_ref, mq_ref, mo_ref)
    for rows, x2 in zip(blocks, x2s):
        o_ref[0, rows, :] = _rmsnorm(x2, fnorm_ref[...])


def _const_spec(shape):
    return pl.BlockSpec(shape, lambda *_: (0,) * len(shape), pipeline_mode=pl.Buffered(1))


def _retention_tables(seq):
    f32 = np.float32
    half = B_DIM // 2
    inv = (1.0 / (f32(ROT_BASE) ** np.linspace(0.0, 1.0, half, dtype=f32))).astype(f32)
    ang = np.arange(seq, dtype=f32)[:, None] * inv[None, :]
    cos, sin = np.cos(ang), np.sin(ang)
    cos2 = np.concatenate([cos, cos], axis=1)
    sin2 = np.concatenate([-sin, sin], axis=1)
    log_gamma = np.log1p(-np.exp2(-5.0 - np.arange(B_HEADS, dtype=f32))).astype(f32)
    idx = np.arange(B_GROUP, dtype=f32)
    chunk_of = np.arange(B_GROUP) // CHUNK
    visible = (chunk_of[None, :] <= chunk_of[:, None]).astype(f32)
    scale = f32(B_DIM ** -0.5)
    dmat = np.exp(log_gamma[:, None, None] * np.abs(idx[:, None] - idx[None, :])) * visible * scale
    qdec = np.exp(log_gamma[:, None] * (idx + 1.0)[None, :])
    kdec = np.exp(log_gamma[:, None] * (B_GROUP - 1 - idx)[None, :]) * scale
    qdec = np.broadcast_to(qdec[:, :, None], (B_HEADS, B_GROUP, B_DIM))
    kdec = np.broadcast_to(kdec[:, :, None], (B_HEADS, B_GROUP, B_DIM))
    gdec = np.broadcast_to(np.exp(log_gamma * B_GROUP)[:, None, None], (B_HEADS, 1, B_DIM))
    return tuple(np.ascontiguousarray(a, dtype=f32) for a in (cos2, sin2, dmat, qdec, kdec, gdec))


def _cast_specs(n_steps, step_of, layers, widths):
    rows = D_MODEL // n_steps
    assert rows * n_steps == D_MODEL and rows % 16 == 0
    ins = [pl.BlockSpec((None, rows, w), lambda *g, l=l: (l, step_of(*g), 0))
           for w, l in zip(widths, layers)]
    outs = [pl.BlockSpec((rows, w), lambda *g: (step_of(*g), 0)) for w in widths]
    shapes = [jax.ShapeDtypeStruct((D_MODEL, w), BF16) for w in widths]
    return ins, outs, shapes


def _forward(x, mem, e_norm, e_w_in, e_sink, e_w_out, o_norm, o_w_in, o_conv_w, o_conv_b,
             o_w_out, c_norm, c_mem_norm, c_wq, c_wkv, c_wo, final_norm):
    batch, seq, d = x.shape
    depth = c_wq.shape[0]
    assert d == D_MODEL and depth == 2 and seq % TS == 0 and mem.shape == (batch, MEM_LEN, D_MODEL)
    n_t = seq // TS
    params = pltpu.CompilerParams(dimension_semantics=("arbitrary", "arbitrary"),
                                  vmem_limit_bytes=VMEM_LIMIT_BYTES)

    xw = X_HEADS * MEM_LEN
    layer_spec = lambda shape, **kw: pl.BlockSpec((None,) + shape, lambda i, b: (i, 0, 0), **kw)
    single = dict(pipeline_mode=pl.Buffered(1))
    c_in, c_out, c_shape = _cast_specs(depth * batch, lambda i, b: i * batch + b, (0, 0),
                                       (EVEN_IN, D_MODEL))
    mq, mo, w_in0, w_out0 = pl.pallas_call(
        _memkv_kernel,
        grid=(depth, batch),
        in_specs=[pl.BlockSpec((1, MEM_LEN, D_MODEL), lambda i, b: (b, 0, 0)),
                  pl.BlockSpec((1, 1, D_MODEL), lambda i, b: (i, 0, 0)),
                  layer_spec((D_MODEL, 2 * D_MODEL)),
                  layer_spec((D_MODEL, D_MODEL), **single),
                  layer_spec((D_MODEL, D_MODEL), **single)] + c_in,
        out_specs=[pl.BlockSpec((1, 1, D_MODEL, xw), lambda i, b: (i, b, 0, 0)),
                   pl.BlockSpec((1, 1, xw, D_MODEL), lambda i, b: (i, b, 0, 0))] + c_out,
        out_shape=[jax.ShapeDtypeStruct((depth, batch, D_MODEL, xw), BF16),
                   jax.ShapeDtypeStruct((depth, batch, xw, D_MODEL), BF16)] + c_shape,
        scratch_shapes=[pltpu.VMEM((D_MODEL, 2 * D_MODEL), BF16),
                        pltpu.VMEM((D_MODEL, D_MODEL), BF16),
                        pltpu.VMEM((D_MODEL, D_MODEL), BF16)],
        compiler_params=params,
        name="memkv",
    )(mem, c_mem_norm.reshape(depth, 1, D_MODEL), c_wkv, c_wq, c_wo, e_w_in, e_w_out)

    cos2, sin2, dmat, qdec, kdec, gdec = _retention_tables(seq)
    sink_rows = jnp.broadcast_to((e_sink[0].astype(F32) * LOG2E)[:, None, None],
                                 (A_Q_HEADS, CHUNK, A_KV_WIDTH)).reshape(A_Q_HEADS * CHUNK, A_KV_WIDTH)
    sink_pad = jnp.where(np.arange(A_KV_WIDTH)[None, :] == A_BAND - A_KV_WIDTH,
                         sink_rows, NEG_INF).astype(F32)

    tile_spec = pl.BlockSpec((1, TS, D_MODEL), lambda b, t: (b, t, 0))
    mq_spec = lambda i: pl.BlockSpec((None, 1, D_MODEL, xw), lambda b, t: (i, b, 0, 0),
                                     pipeline_mode=pl.Buffered(1))
    mo_spec = lambda i: pl.BlockSpec((None, 1, xw, D_MODEL), lambda b, t: (i, b, 0, 0),
                                     pipeline_mode=pl.Buffered(1))
    c_in, c_out, c_shape = _cast_specs(batch * n_t, lambda b, t: b * n_t + t, (0, 0),
                                       (4 * D_MODEL, D_MODEL))

    x1, w_in1, w_out1 = pl.pallas_call(
        _layer0_kernel,
        grid=(batch, n_t),
        in_specs=[tile_spec,
                  pl.BlockSpec((TS, B_DIM), lambda b, t: (t, 0)),
                  pl.BlockSpec((TS, B_DIM), lambda b, t: (t, 0)),
                  _const_spec((B_HEADS, B_GROUP, B_GROUP)),
                  _const_spec((B_HEADS, B_GROUP, B_DIM)),
                  _const_spec((B_HEADS, B_GROUP, B_DIM)),
                  _const_spec((B_HEADS, 1, B_DIM)),
                  _const_spec((A_Q_HEADS * CHUNK, A_KV_WIDTH)),
                  _const_spec((1, D_MODEL)),
                  _const_spec((D_MODEL, EVEN_IN)),
                  _const_spec((MIX_WIDTH, D_MODEL)),
                  _const_spec((1, D_MODEL)),
                  mq_spec(0), mo_spec(0)] + c_in,
        out_specs=[tile_spec] + c_out,
        out_shape=[jax.ShapeDtypeStruct((batch, seq, D_MODEL), F32)] + c_shape,
        scratch_shapes=[pltpu.VMEM((TS, EVEN_IN), F32),
                        pltpu.VMEM((TS, MIX_WIDTH), BF16),
                        pltpu.VMEM((A_KV_HEADS, A_TAIL + TS, A_KV_WIDTH), BF16),
                        pltpu.VMEM((A_KV_HEADS, A_TAIL + TS, 2 * A_KV_WIDTH), BF16),
                        pltpu.VMEM((B_HEADS, B_DIM, B_DIM), F32)],
        compiler_params=params,
        name="layer0",
    )(x, cos2, sin2, dmat, qdec, kdec, gdec, sink_pad,
      e_norm[0].reshape(1, D_MODEL), w_in0, w_out0,
      c_norm[0].reshape(1, D_MODEL), mq, mo,
      o_w_in, o_w_out)

    out = pl.pallas_call(
        _layer1_kernel,
        grid=(batch, n_t),
        in_specs=[tile_spec,
                  _const_spec((1, D_MODEL)),
                  _const_spec((D_MODEL, 4 * D_MODEL)),
                  _const_spec((C_CONV, D_MODEL)),
                  _const_spec((1, D_MODEL)),
                  _const_spec((D_MODEL, D_MODEL)),
                  _const_spec((1, D_MODEL)),
                  mq_spec(1), mo_spec(1),
                  _const_spec((1, D_MODEL))],
        out_specs=tile_spec,
        out_shape=jax.ShapeDtypeStruct((batch, seq, D_MODEL), F32),
        scratch_shapes=[pltpu.VMEM((TS, 4 * D_MODEL), F32),
                        pltpu.VMEM((C_PAD + TS, D_MODEL), F32)],
        compiler_params=params,
        name="layer1",
    )(x1, o_norm[0].reshape(1, D_MODEL), w_in1, o_conv_w[0],
      o_conv_b[0].reshape(1, D_MODEL), w_out1,
      c_norm[1].reshape(1, D_MODEL), mq, mo, final_norm.reshape(1, D_MODEL))
    return x1, out


def kernel(x, mem, e_norm, e_w_in, e_sink, e_w_out, o_norm, o_w_in, o_conv_w, o_conv_b,
           o_w_out, c_norm, c_mem_norm, c_wq, c_wkv, c_wo, final_norm):
    return _forward(x, mem, e_norm, e_w_in, e_sink, e_w_out, o_norm, o_w_in, o_conv_w, o_conv_b,
                    o_w_out, c_norm, c_mem_norm, c_wq, c_wkv, c_wo, final_norm)[1]
```

```python
import jax
import jax.numpy as jnp
import numpy as np
from jax import lax
from jax.experimental import pallas as pl
from jax.experimental.pallas import tpu as pltpu

F32 = jnp.float32
BF16 = jnp.bfloat16

D_MODEL = 1024
CHUNK = 64
MEM_LEN = 256
EPS = 1e-6
NEG_INF = -1e30
LOG2E = 1.4426950408889634

A_Q_HEADS = 8
A_KV_HEADS = 2
A_HEAD_DIM = 64
A_WIN_CHUNKS = 2
A_WIDTH = A_Q_HEADS * A_HEAD_DIM
A_KV_WIDTH = A_KV_HEADS * A_HEAD_DIM
A_BAND = (A_WIN_CHUNKS + 1) * CHUNK
A_TAIL = A_WIN_CHUNKS * CHUNK
A_SCALE = A_HEAD_DIM ** -0.5 * LOG2E

B_HEADS = 4
B_DIM = 128
B_WIDTH = B_HEADS * B_DIM
ROT_BASE = 10000.0
B_GROUP = 256

C_CONV = 3
C_PAD = 8

X_HEADS = 4
X_DIM = D_MODEL // X_HEADS
X_SCALE = X_DIM ** -0.5 * LOG2E

OFF_AQ = 0
OFF_AK = OFF_AQ + A_WIDTH
OFF_AV = OFF_AK + A_KV_WIDTH
OFF_AG = OFF_AV + A_KV_WIDTH
OFF_BQ = OFF_AG + A_WIDTH
OFF_BK = OFF_BQ + B_WIDTH
OFF_BV = OFF_BK + B_WIDTH
OFF_BG = OFF_BV + B_WIDTH
EVEN_IN = OFF_BG + B_WIDTH
MIX_WIDTH = A_WIDTH + B_WIDTH

TS = 1024
SUB0 = 256
SUB1 = 256
VMEM_LIMIT_BYTES = 56 * 1024 * 1024
MEMKV_VMEM_LIMIT_BYTES = 60 * 1024 * 1024


def _rmsnorm(x, g):
    return x * lax.rsqrt(jnp.mean(x * x, axis=-1, keepdims=True) + EPS) * g


def _silu(x):
    return x * jax.nn.sigmoid(x)


def _dot(a, b):
    return jnp.dot(a, b, preferred_element_type=F32)


def _dot_nt(a, b):
    return lax.dot_general(a, b, (((1,), (1,)), ((), ())), preferred_element_type=F32)


def _cast_blocks(pairs):
    for src_ref, dst_ref in pairs:
        dst_ref[...] = src_ref[...].astype(BF16)


def _memkv_kernel(mem_ref, g_ref, wkv_ref, wq_ref, wo_ref, win_ref, wout_ref,
                  mq_ref, mo_ref, win_bf_ref, wout_bf_ref, wkv_bf_ref, wq_bf_ref, wo_bf_ref):
    @pl.when(pl.program_id(1) == 0)
    def _():
        wkv_bf_ref[...] = wkv_ref[...].astype(BF16)
        wq_bf_ref[...] = wq_ref[...].astype(BF16)
        wo_bf_ref[...] = wo_ref[...].astype(BF16)

    mn = _rmsnorm(mem_ref[0], g_ref[0]).astype(BF16)
    kv = _dot(mn, wkv_bf_ref[...])
    k = (kv[:, :D_MODEL] * X_SCALE).astype(BF16)
    v = kv[:, D_MODEL:].astype(BF16)
    for h in range(X_HEADS):
        cs = slice(h * X_DIM, (h + 1) * X_DIM)
        mq_ref[0, 0, :, h * MEM_LEN:(h + 1) * MEM_LEN] = _dot_nt(wq_bf_ref[:, cs], k[:, cs]).astype(BF16)
        mo_ref[0, 0, h * MEM_LEN:(h + 1) * MEM_LEN, :] = _dot(v[:, cs], wo_bf_ref[cs, :]).astype(BF16)
    _cast_blocks([(win_ref, win_bf_ref), (wout_ref, wout_bf_ref)])


def _softmax_heads(sc):
    heads = []
    for h in range(X_HEADS):
        s = sc[:, h * MEM_LEN:(h + 1) * MEM_LEN]
        p = jnp.exp2(s - jnp.max(s, axis=-1, keepdims=True))
        heads.append((p * (1.0 / jnp.sum(p, axis=-1, keepdims=True))).astype(BF16))
    return jnp.concatenate(heads, axis=1)


def _cross_attention_blocks(x1s, cnorm_ref, mq_ref, mo_ref):
    scores = [_dot(_rmsnorm(x1, cnorm_ref[...]).astype(BF16), mq_ref[0]) for x1 in x1s]
    probs = [_softmax_heads(sc) for sc in scores]
    return [x1 + _dot(p, mo_ref[0]) for x1, p in zip(x1s, probs)]


def _layer0_kernel(x_ref, cos_ref, sin_ref, dmat_ref, qdec_ref, kdec_ref, gdec_ref, sink_ref,
                   enorm_ref, win_ref, wout_ref, cnorm_ref, mq_ref, mo_ref,
                   nwin_ref, nwout_ref,
                   o_ref, nwin_bf_ref, nwout_bf_ref,
                   h_ref, y_ref, k_ref, v_ref, state_ref):
    t = pl.program_id(1)
    _cast_blocks([(nwin_ref, nwin_bf_ref), (nwout_ref, nwout_bf_ref)])

    @pl.when(t == 0)
    def _():
        k_ref[:, 0:A_TAIL, :] = jnp.zeros((A_KV_HEADS, A_TAIL, A_KV_WIDTH), BF16)
        v_ref[:, 0:A_TAIL, 0:A_KV_WIDTH] = jnp.zeros((A_KV_HEADS, A_TAIL, A_KV_WIDTH), BF16)
        v_ref[:, :, A_KV_WIDTH:] = jnp.ones((A_KV_HEADS, A_TAIL + TS, A_KV_WIDTH), BF16)
        state_ref[...] = jnp.zeros_like(state_ref)

    @pl.when(t > 0)
    def _():
        k_ref[:, 0:A_TAIL, :] = k_ref[:, TS:TS + A_TAIL, :]
        v_ref[:, 0:A_TAIL, 0:A_KV_WIDTH] = v_ref[:, TS:TS + A_TAIL, 0:A_KV_WIDTH]

    lo = lax.broadcasted_iota(jnp.int32, (CHUNK, A_KV_WIDTH), 1) < A_HEAD_DIM
    lo_blk = lax.broadcasted_iota(jnp.int32, (SUB0, A_KV_WIDTH), 1) < A_HEAD_DIM
    kidx = lax.broadcasted_iota(jnp.int32, (4 * CHUNK, A_KV_WIDTH), 1)
    hi_half = kidx >= A_BAND - A_KV_WIDTH
    pad_k = jnp.zeros((2 * A_KV_WIDTH - A_BAND, A_KV_WIDTH), BF16)
    pad_r = lax.broadcasted_iota(jnp.int32, (2 * A_KV_WIDTH - A_BAND, 2 * A_KV_WIDTH), 0)
    pad_c = lax.broadcasted_iota(jnp.int32, (2 * A_KV_WIDTH - A_BAND, 2 * A_KV_WIDTH), 1)
    pad_v = jnp.where((pad_r == 0) & (pad_c >= A_KV_WIDTH), 1.0, 0.0).astype(BF16)

    def store_kv(rows):
        hist = slice(A_TAIL + rows.start, A_TAIL + rows.stop)
        kk = h_ref[rows, OFF_AK:OFF_AK + A_KV_WIDTH] * A_SCALE
        vv = h_ref[rows, OFF_AV:OFF_AV + A_KV_WIDTH]
        kr = pltpu.roll(kk, A_HEAD_DIM, 1)
        vr = pltpu.roll(vv, A_HEAD_DIM, 1)
        k_ref[0, hist, :] = jnp.where(lo_blk, kk, kr).astype(BF16)
        k_ref[1, hist, :] = jnp.where(lo_blk, kr, kk).astype(BF16)
        v_ref[0, hist, 0:A_KV_WIDTH] = jnp.where(lo_blk, vv, vr).astype(BF16)
        v_ref[1, hist, 0:A_KV_WIDTH] = jnp.where(lo_blk, vr, vv).astype(BF16)

    def swa_scores(c):
        r0 = c * CHUNK
        q = h_ref[r0:r0 + CHUNK, OFF_AQ:OFF_AQ + A_WIDTH]
        out = []
        for g in range(A_KV_HEADS):
            qa, qb = q[:, (2 * g) * 128:(2 * g + 1) * 128], q[:, (2 * g + 1) * 128:(2 * g + 2) * 128]
            zero = jnp.zeros_like(qa)
            lhs = jnp.concatenate([jnp.where(lo, qa, zero), jnp.where(lo, zero, qa),
                                   jnp.where(lo, qb, zero), jnp.where(lo, zero, qb)],
                                  axis=0).astype(BF16)
            out.append(_dot_nt(lhs, jnp.concatenate([k_ref[g, r0:r0 + A_BAND, :], pad_k], axis=0)))
        return out

    def swa_probs(c, scores):
        out = []
        for g, s in enumerate(scores):
            s0 = s[:, :A_KV_WIDTH]
            s1 = jnp.where(hi_half, sink_ref[g * 4 * CHUNK:(g + 1) * 4 * CHUNK, :],
                           s[:, A_KV_WIDTH:])
            if c < A_WIN_CHUNKS:
                n_masked = jnp.where(t == 0, (A_WIN_CHUNKS - c) * CHUNK, 0)
                s0 = jnp.where(kidx < n_masked, NEG_INF, s0)
            m = jnp.max(jnp.maximum(s0, s1), axis=-1, keepdims=True)
            out.append(jnp.concatenate([jnp.exp2(s0 - m), jnp.exp2(s1 - m)], axis=1).astype(BF16))
        return out

    def swa_values(c, probs):
        r0 = c * CHUNK
        return [_dot(p, jnp.concatenate([v_ref[g, r0:r0 + A_BAND, :], pad_v], axis=0))
                for g, p in enumerate(probs)]

    def swa_finish(c, pvs):
        r0 = c * CHUNK
        cols = []
        for pv in pvs:
            o = pv[:, :A_KV_WIDTH] / pv[:, A_KV_WIDTH:]
            cols += [jnp.where(lo, o[0:CHUNK], o[CHUNK:2 * CHUNK]),
                     jnp.where(lo, o[2 * CHUNK:3 * CHUNK], o[3 * CHUNK:4 * CHUNK])]
        ya = jnp.concatenate(cols, axis=1)
        ag = h_ref[r0:r0 + CHUNK, OFF_AG:OFF_AG + A_WIDTH]
        y_ref[r0:r0 + CHUNK, 0:A_WIDTH] = (ya * _silu(ag)).astype(BF16)

    def ret_scores(rows, h):
        cos = cos_ref[rows, :]
        sin = sin_ref[rows, :]
        q = h_ref[rows, OFF_BQ + h * B_DIM:OFF_BQ + (h + 1) * B_DIM]
        k = h_ref[rows, OFF_BK + h * B_DIM:OFF_BK + (h + 1) * B_DIM]
        qr = q * cos + pltpu.roll(q, B_DIM // 2, 1) * sin
        kr = k * cos + pltpu.roll(k, B_DIM // 2, 1) * sin
        return qr, kr, _dot_nt(qr.astype(BF16), kr.astype(BF16))

    def ret_output(rows, h, qr, sc):
        v = h_ref[rows, OFF_BV + h * B_DIM:OFF_BV + (h + 1) * B_DIM].astype(BF16)
        lhs = jnp.concatenate([(sc * dmat_ref[h]).astype(BF16), (qr * qdec_ref[h]).astype(BF16)], axis=1)
        rhs = jnp.concatenate([v, state_ref[h].astype(BF16)], axis=0)
        return _dot(lhs, rhs)

    def ret_state(rows, h, kr):
        v = h_ref[rows, OFF_BV + h * B_DIM:OFF_BV + (h + 1) * B_DIM].astype(BF16)
        kd = (kr * kdec_ref[h]).T.astype(BF16)
        state_ref[h] = state_ref[h] * gdec_ref[h] + _dot(kd, v)

    def ret_finish(rows, h, o):
        o = o * lax.rsqrt(jnp.mean(o * o, axis=-1, keepdims=True) + EPS)
        bg = h_ref[rows, OFF_BG + h * B_DIM:OFF_BG + (h + 1) * B_DIM]
        y_ref[rows, A_WIDTH + h * B_DIM:A_WIDTH + (h + 1) * B_DIM] = (o * _silu(bg)).astype(BF16)

    def mix_block(rows):
        chunks = range(rows.start // CHUNK, rows.stop // CHUNK)
        scores = [swa_scores(c) for c in chunks]
        probs = [swa_probs(c, s) for c, s in zip(chunks, scores)]
        pvs = [swa_values(c, p) for c, p in zip(chunks, probs)]
        for c, pv in zip(chunks, pvs):
            swa_finish(c, pv)
        for g0 in range(rows.start, rows.stop, B_GROUP):
            grp = slice(g0, g0 + B_GROUP)
            qks = [ret_scores(grp, h) for h in range(B_HEADS)]
            outs = [ret_output(grp, h, qr, sc) for h, (qr, _, sc) in enumerate(qks)]
            for h, (_, kr, _) in enumerate(qks):
                ret_state(grp, h, kr)
            for h, o in enumerate(outs):
                ret_finish(grp, h, o)

    blocks = [slice(s0, s0 + SUB0) for s0 in range(0, TS, SUB0)]
    xs = [x_ref[0, rows, :] for rows in blocks]
    for rows, x in zip(blocks, xs):
        xn = _rmsnorm(x, enorm_ref[...]).astype(BF16)
        h_ref[rows, :] = _dot(xn, win_ref[...])
        store_kv(rows)
    x1s = []
    for rows, x in zip(blocks, xs):
        mix_block(rows)
        x1s.append(x + _dot(y_ref[rows, :], wout_ref[...]))
    x2s = _cross_attention_blocks(x1s, cnorm_ref, mq_ref, mo_ref)
    for rows, x2 in zip(blocks, x2s):
        o_ref[0, rows, :] = x2


def _layer1_kernel(x_ref, onorm_ref, win_ref, convw_ref, convb_ref, wout_ref,
                   cnorm_ref, mq_ref, mo_ref, fnorm_ref,
                   o_ref, h_ref, z_ref):
    t = pl.program_id(1)

    @pl.when(t == 0)
    def _():
        z_ref[0:C_PAD, :] = jnp.zeros((C_PAD, D_MODEL), F32)

    @pl.when(t > 0)
    def _():
        z_ref[0:C_PAD, :] = z_ref[TS:TS + C_PAD, :]

    blocks = [slice(s0, s0 + SUB1) for s0 in range(0, TS, SUB1)]
    xs = [x_ref[0, rows, :] for rows in blocks]
    for rows, x in zip(blocks, xs):
        xn = _rmsnorm(x, onorm_ref[...]).astype(BF16)
        h_ref[rows, :] = _dot(xn, win_ref[...])
    x1s = []
    for rows, x in zip(blocks, xs):
        zrows = lambda back: slice(C_PAD - back + rows.start, C_PAD - back + rows.stop)
        z = h_ref[rows, D_MODEL:2 * D_MODEL] * h_ref[rows, 2 * D_MODEL:3 * D_MODEL]
        z_ref[zrows(0), :] = z
        conv = convb_ref[...] + z * convw_ref[2:3, :]
        conv = conv + z_ref[zrows(1), :] * convw_ref[1:2, :]
        conv = conv + z_ref[zrows(2), :] * convw_ref[0:1, :]
        y = h_ref[rows, 0:D_MODEL] * conv * _silu(h_ref[rows, 3 * D_MODEL:4 * D_MODEL])
        x1s.append(x + _dot(y.astype(BF16), wout_ref[...]))
    x2s = _cross_attention_blocks(x1s, cnorm_ref, mq_ref, mo_ref)
    for rows, x2 in zip(blocks, x2s):
        o_ref[0, rows, :] = _rmsnorm(x2, fnorm_ref[...])


def _const_spec(shape):
    return pl.BlockSpec(shape, lambda *_: (0,) * len(shape), pipeline_mode=pl.Buffered(1))


def _retention_tables(seq):
    f32 = np.float32
    half = B_DIM // 2
    inv = (1.0 / (f32(ROT_BASE) ** np.linspace(0.0, 1.0, half, dtype=f32))).astype(f32)
    ang = np.arange(seq, dtype=f32)[:, None] * inv[None, :]
    cos, sin = np.cos(ang), np.sin(ang)
    cos2 = np.concatenate([cos, cos], axis=1)
    sin2 = np.concatenate([-sin, sin], axis=1)
    log_gamma = np.log1p(-np.exp2(-5.0 - np.arange(B_HEADS, dtype=f32))).astype(f32)
    idx = np.arange(B_GROUP, dtype=f32)
    chunk_of = np.arange(B_GROUP) // CHUNK
    visible = (chunk_of[None, :] <= chunk_of[:, None]).astype(f32)
    scale = f32(B_DIM ** -0.5)
    dmat = np.exp(log_gamma[:, None, None] * np.abs(idx[:, None] - idx[None, :])) * visible * scale
    qdec = np.exp(log_gamma[:, None] * (idx + 1.0)[None, :])
    kdec = np.exp(log_gamma[:, None] * (B_GROUP - 1 - idx)[None, :]) * scale
    qdec = np.broadcast_to(qdec[:, :, None], (B_HEADS, B_GROUP, B_DIM))
    kdec = np.broadcast_to(kdec[:, :, None], (B_HEADS, B_GROUP, B_DIM))
    gdec = np.broadcast_to(np.exp(log_gamma * B_GROUP)[:, None, None], (B_HEADS, 1, B_DIM))
    return tuple(np.ascontiguousarray(a, dtype=f32) for a in (cos2, sin2, dmat, qdec, kdec, gdec))


def _cast_specs(n_steps, step_of, layers, widths):
    rows = D_MODEL // n_steps
    assert rows * n_steps == D_MODEL and rows % 16 == 0
    ins = [pl.BlockSpec((None, rows, w), lambda *g, l=l: (l, step_of(*g), 0))
           for w, l in zip(widths, layers)]
    outs = [pl.BlockSpec((rows, w), lambda *g: (step_of(*g), 0)) for w in widths]
    shapes = [jax.ShapeDtypeStruct((D_MODEL, w), BF16) for w in widths]
    return ins, outs, shapes


def _forward(x, mem, e_norm, e_w_in, e_sink, e_w_out, o_norm, o_w_in, o_conv_w, o_conv_b,
             o_w_out, c_norm, c_mem_norm, c_wq, c_wkv, c_wo, final_norm):
    batch, seq, d = x.shape
    depth = c_wq.shape[0]
    assert d == D_MODEL and depth == 2 and seq % TS == 0 and mem.shape == (batch, MEM_LEN, D_MODEL)
    n_t = seq // TS
    params = pltpu.CompilerParams(dimension_semantics=("arbitrary", "arbitrary"),
                                  vmem_limit_bytes=VMEM_LIMIT_BYTES)

    xw = X_HEADS * MEM_LEN
    layer_spec = lambda shape, **kw: pl.BlockSpec((None,) + shape, lambda i, b: (i, 0, 0), **kw)
    c_in, c_out, c_shape = _cast_specs(depth * batch, lambda i, b: i * batch + b, (0, 0),
                                       (EVEN_IN, D_MODEL))
    mq, mo, w_in0, w_out0 = pl.pallas_call(
        _memkv_kernel,
        grid=(depth, batch),
        in_specs=[pl.BlockSpec((1, MEM_LEN, D_MODEL), lambda i, b: (b, 0, 0)),
                  pl.BlockSpec((1, 1, D_MODEL), lambda i, b: (i, 0, 0)),
                  layer_spec((D_MODEL, 2 * D_MODEL)),
                  layer_spec((D_MODEL, D_MODEL)),
                  layer_spec((D_MODEL, D_MODEL))] + c_in,
        out_specs=[pl.BlockSpec((1, 1, D_MODEL, xw), lambda i, b: (i, b, 0, 0)),
                   pl.BlockSpec((1, 1, xw, D_MODEL), lambda i, b: (i, b, 0, 0))] + c_out,
        out_shape=[jax.ShapeDtypeStruct((depth, batch, D_MODEL, xw), BF16),
                   jax.ShapeDtypeStruct((depth, batch, xw, D_MODEL), BF16)] + c_shape,
        scratch_shapes=[pltpu.VMEM((D_MODEL, 2 * D_MODEL), BF16),
                        pltpu.VMEM((D_MODEL, D_MODEL), BF16),
                        pltpu.VMEM((D_MODEL, D_MODEL), BF16)],
        compiler_params=pltpu.CompilerParams(dimension_semantics=("arbitrary", "arbitrary"),
                                             vmem_limit_bytes=MEMKV_VMEM_LIMIT_BYTES),
        name="memkv",
    )(mem, c_mem_norm.reshape(depth, 1, D_MODEL), c_wkv, c_wq, c_wo, e_w_in, e_w_out)

    cos2, sin2, dmat, qdec, kdec, gdec = _retention_tables(seq)
    sink_rows = jnp.broadcast_to((e_sink[0].astype(F32) * LOG2E)[:, None, None],
                                 (A_Q_HEADS, CHUNK, A_KV_WIDTH)).reshape(A_Q_HEADS * CHUNK, A_KV_WIDTH)
    sink_pad = jnp.where(np.arange(A_KV_WIDTH)[None, :] == A_BAND - A_KV_WIDTH,
                         sink_rows, NEG_INF).astype(F32)

    tile_spec = pl.BlockSpec((1, TS, D_MODEL), lambda b, t: (b, t, 0))
    mq_spec = lambda i: pl.BlockSpec((None, 1, D_MODEL, xw), lambda b, t: (i, b, 0, 0))
    mo_spec = lambda i: pl.BlockSpec((None, 1, xw, D_MODEL), lambda b, t: (i, b, 0, 0))
    c_in, c_out, c_shape = _cast_specs(batch * n_t, lambda b, t: b * n_t + t, (0, 0),
                                       (4 * D_MODEL, D_MODEL))

    x1, w_in1, w_out1 = pl.pallas_call(
        _layer0_kernel,
        grid=(batch, n_t),
        in_specs=[tile_spec,
                  pl.BlockSpec((TS, B_DIM), lambda b, t: (t, 0)),
                  pl.BlockSpec((TS, B_DIM), lambda b, t: (t, 0)),
                  _const_spec((B_HEADS, B_GROUP, B_GROUP)),
                  _const_spec((B_HEADS, B_GROUP, B_DIM)),
                  _const_spec((B_HEADS, B_GROUP, B_DIM)),
                  _const_spec((B_HEADS, 1, B_DIM)),
                  _const_spec((A_Q_HEADS * CHUNK, A_KV_WIDTH)),
                  _const_spec((1, D_MODEL)),
                  _const_spec((D_MODEL, EVEN_IN)),
                  _const_spec((MIX_WIDTH, D_MODEL)),
                  _const_spec((1, D_MODEL)),
                  mq_spec(0), mo_spec(0)] + c_in,
        out_specs=[tile_spec] + c_out,
        out_shape=[jax.ShapeDtypeStruct((batch, seq, D_MODEL), F32)] + c_shape,
        scratch_shapes=[pltpu.VMEM((TS, EVEN_IN), F32),
                        pltpu.VMEM((TS, MIX_WIDTH), BF16),
                        pltpu.VMEM((A_KV_HEADS, A_TAIL + TS, A_KV_WIDTH), BF16),
                        pltpu.VMEM((A_KV_HEADS, A_TAIL + TS, 2 * A_KV_WIDTH), BF16),
                        pltpu.VMEM((B_HEADS, B_DIM, B_DIM), F32)],
        compiler_params=params,
        name="layer0",
    )(x, cos2, sin2, dmat, qdec, kdec, gdec, sink_pad,
      e_norm[0].reshape(1, D_MODEL), w_in0, w_out0,
      c_norm[0].reshape(1, D_MODEL), mq, mo,
      o_w_in, o_w_out)

    out = pl.pallas_call(
        _layer1_kernel,
        grid=(batch, n_t),
        in_specs=[tile_spec,
                  _const_spec((1, D_MODEL)),
                  _const_spec((D_MODEL, 4 * D_MODEL)),
                  _const_spec((C_CONV, D_MODEL)),
                  _const_spec((1, D_MODEL)),
                  _const_spec((D_MODEL, D_MODEL)),
                  _const_spec((1, D_MODEL)),
                  mq_spec(1), mo_spec(1),
                  _const_spec((1, D_MODEL))],
        out_specs=tile_spec,
        out_shape=jax.ShapeDtypeStruct((batch, seq, D_MODEL), F32),
        scratch_shapes=[pltpu.VMEM((TS, 4 * D_MODEL), F32),
                        pltpu.VMEM((C_PAD + TS, D_MODEL), F32)],
        compiler_params=params,
        name="layer1",
    )(x1, o_norm[0].reshape(1, D_MODEL), w_in1, o_conv_w[0],
      o_conv_b[0].reshape(1, D_MODEL), w_out1,
      c_norm[1].reshape(1, D_MODEL), mq, mo, final_norm.reshape(1, D_MODEL))
    return x1, out


def kernel(x, mem, e_norm, e_w_in, e_sink, e_w_out, o_norm, o_w_in, o_conv_w, o_conv_b,
           o_w_out, c_norm, c_mem_norm, c_wq, c_wkv, c_wo, final_norm):
    return _forward(x, mem, e_norm, e_w_in, e_sink, e_w_out, o_norm, o_w_in, o_conv_w, o_conv_b,
                    o_w_out, c_norm, c_mem_norm, c_wq, c_wkv, c_wo, final_norm)[1]
```

```python
import jax
import jax.numpy as jnp
import numpy as np
from jax import lax
from jax.experimental import pallas as pl
from jax.experimental.pallas import tpu as pltpu

F32 = jnp.float32
BF16 = jnp.bfloat16

D_MODEL = 1024
CHUNK = 64
MEM_LEN = 256
EPS = 1e-6
NEG_INF = -1e30
LOG2E = 1.4426950408889634

A_Q_HEADS = 8
A_KV_HEADS = 2
A_HEAD_DIM = 64
A_WIN_CHUNKS = 2
A_WIDTH = A_Q_HEADS * A_HEAD_DIM
A_KV_WIDTH = A_KV_HEADS * A_HEAD_DIM
A_BAND = (A_WIN_CHUNKS + 1) * CHUNK
A_TAIL = A_WIN_CHUNKS * CHUNK
A_SCALE = A_HEAD_DIM ** -0.5 * LOG2E

B_HEADS = 4
B_DIM = 128
B_WIDTH = B_HEADS * B_DIM
ROT_BASE = 10000.0
B_GROUP = 256

C_CONV = 3
C_PAD = 8

X_HEADS = 4
X_DIM = D_MODEL // X_HEADS
X_SCALE = X_DIM ** -0.5 * LOG2E

OFF_AQ = 0
OFF_AK = OFF_AQ + A_WIDTH
OFF_AV = OFF_AK + A_KV_WIDTH
OFF_AG = OFF_AV + A_KV_WIDTH
OFF_BQ = OFF_AG + A_WIDTH
OFF_BK = OFF_BQ + B_WIDTH
OFF_BV = OFF_BK + B_WIDTH
OFF_BG = OFF_BV + B_WIDTH
EVEN_IN = OFF_BG + B_WIDTH
MIX_WIDTH = A_WIDTH + B_WIDTH

TS = 1024
SUB0 = 256
SUB1 = 256
VMEM_LIMIT_BYTES = 56 * 1024 * 1024
MEMKV_VMEM_LIMIT_BYTES = 60 * 1024 * 1024


def _rmsnorm(x, g):
    return x * lax.rsqrt(jnp.mean(x * x, axis=-1, keepdims=True) + EPS) * g


def _silu(x):
    return x * jax.nn.sigmoid(x)


def _dot(a, b):
    return jnp.dot(a, b, preferred_element_type=F32)


def _dot_nt(a, b):
    return lax.dot_general(a, b, (((1,), (1,)), ((), ())), preferred_element_type=F32)


def _cast_blocks(pairs):
    for src_ref, dst_ref in pairs:
        dst_ref[...] = src_ref[...].astype(BF16)


def _memkv_kernel(mem_ref, g_ref, wkv_ref, wq_ref, wo_ref, win_ref, wout_ref,
                  mq_ref, mo_ref, win_bf_ref, wout_bf_ref, wkv_bf_ref, wq_bf_ref, wo_bf_ref):
    @pl.when(pl.program_id(1) == 0)
    def _():
        wkv_bf_ref[...] = wkv_ref[...].astype(BF16)
        wq_bf_ref[...] = wq_ref[...].astype(BF16)
        wo_bf_ref[...] = wo_ref[...].astype(BF16)

    mn = _rmsnorm(mem_ref[0], g_ref[0]).astype(BF16)
    kv = _dot(mn, wkv_bf_ref[...])
    k = (kv[:, :D_MODEL] * X_SCALE).astype(BF16)
    v = kv[:, D_MODEL:].astype(BF16)
    for h in range(X_HEADS):
        cs = slice(h * X_DIM, (h + 1) * X_DIM)
        mq_ref[0, 0, :, h * MEM_LEN:(h + 1) * MEM_LEN] = _dot_nt(wq_bf_ref[:, cs], k[:, cs]).astype(BF16)
        mo_ref[0, 0, h * MEM_LEN:(h + 1) * MEM_LEN, :] = _dot(v[:, cs], wo_bf_ref[cs, :]).astype(BF16)
    _cast_blocks([(win_ref, win_bf_ref), (wout_ref, wout_bf_ref)])


def _softmax_heads(sc):
    heads = []
    for h in range(X_HEADS):
        s = sc[:, h * MEM_LEN:(h + 1) * MEM_LEN]
        p = jnp.exp2(s - jnp.max(s, axis=-1, keepdims=True))
        heads.append((p * (1.0 / jnp.sum(p, axis=-1, keepdims=True))).astype(BF16))
    return jnp.concatenate(heads, axis=1)


def _cross_attention_blocks(x1s, cnorm_ref, mq_ref, mo_ref):
    scores = [_dot(_rmsnorm(x1, cnorm_ref[...]).astype(BF16), mq_ref[0]) for x1 in x1s]
    probs = [_softmax_heads(sc) for sc in scores]
    return [x1 + _dot(p, mo_ref[0]) for x1, p in zip(x1s, probs)]


def _layer0_kernel(x_ref, cos_ref, sin_ref, dmat_ref, qdec_ref, kdec_ref, gdec_ref, sink_ref,
                   enorm_ref, win_ref, wout_ref, cnorm_ref, mq_ref, mo_ref,
                   nwin_ref, nwout_ref,
                   o_ref, nwin_bf_ref, nwout_bf_ref,
                   h_ref, y_ref, k_ref, v_ref, state_ref):
    t = pl.program_id(1)
    _cast_blocks([(nwin_ref, nwin_bf_ref), (nwout_ref, nwout_bf_ref)])

    @pl.when(t == 0)
    def _():
        k_ref[:, 0:A_TAIL, :] = jnp.zeros((A_KV_HEADS, A_TAIL, A_KV_WIDTH), BF16)
        v_ref[:, 0:A_TAIL, 0:A_KV_WIDTH] = jnp.zeros((A_KV_HEADS, A_TAIL, A_KV_WIDTH), BF16)
        v_ref[:, :, A_KV_WIDTH:] = jnp.ones((A_KV_HEADS, A_TAIL + TS, A_KV_WIDTH), BF16)
        state_ref[...] = jnp.zeros_like(state_ref)

    @pl.when(t > 0)
    def _():
        k_ref[:, 0:A_TAIL, :] = k_ref[:, TS:TS + A_TAIL, :]
        v_ref[:, 0:A_TAIL, 0:A_KV_WIDTH] = v_ref[:, TS:TS + A_TAIL, 0:A_KV_WIDTH]

    lo = lax.broadcasted_iota(jnp.int32, (CHUNK, A_KV_WIDTH), 1) < A_HEAD_DIM
    lo_blk = lax.broadcasted_iota(jnp.int32, (SUB0, A_KV_WIDTH), 1) < A_HEAD_DIM
    kidx = lax.broadcasted_iota(jnp.int32, (4 * CHUNK, A_KV_WIDTH), 1)
    hi_half = kidx >= A_BAND - A_KV_WIDTH
    pad_k = jnp.zeros((2 * A_KV_WIDTH - A_BAND, A_KV_WIDTH), BF16)
    pad_r = lax.broadcasted_iota(jnp.int32, (2 * A_KV_WIDTH - A_BAND, 2 * A_KV_WIDTH), 0)
    pad_c = lax.broadcasted_iota(jnp.int32, (2 * A_KV_WIDTH - A_BAND, 2 * A_KV_WIDTH), 1)
    pad_v = jnp.where((pad_r == 0) & (pad_c >= A_KV_WIDTH), 1.0, 0.0).astype(BF16)

    def store_kv(rows):
        hist = slice(A_TAIL + rows.start, A_TAIL + rows.stop)
        kk = h_ref[rows, OFF_AK:OFF_AK + A_KV_WIDTH] * A_SCALE
        vv = h_ref[rows, OFF_AV:OFF_AV + A_KV_WIDTH]
        kr = pltpu.roll(kk, A_HEAD_DIM, 1)
        vr = pltpu.roll(vv, A_HEAD_DIM, 1)
        k_ref[0, hist, :] = jnp.where(lo_blk, kk, kr).astype(BF16)
        k_ref[1, hist, :] = jnp.where(lo_blk, kr, kk).astype(BF16)
        v_ref[0, hist, 0:A_KV_WIDTH] = jnp.where(lo_blk, vv, vr).astype(BF16)
        v_ref[1, hist, 0:A_KV_WIDTH] = jnp.where(lo_blk, vr, vv).astype(BF16)

    def swa_scores(c):
        r0 = c * CHUNK
        q = h_ref[r0:r0 + CHUNK, OFF_AQ:OFF_AQ + A_WIDTH]
        out = []
        for g in range(A_KV_HEADS):
            w = 2 * A_HEAD_DIM
            qa, qb = q[:, (2 * g) * w:(2 * g + 1) * w], q[:, (2 * g + 1) * w:(2 * g + 2) * w]
            zero = jnp.zeros_like(qa)
            lhs = jnp.concatenate([jnp.where(lo, qa, zero), jnp.where(lo, zero, qa),
                                   jnp.where(lo, qb, zero), jnp.where(lo, zero, qb)],
                                  axis=0).astype(BF16)
            out.append(_dot_nt(lhs, jnp.concatenate([k_ref[g, r0:r0 + A_BAND, :], pad_k], axis=0)))
        return out

    def swa_probs(c, scores):
        out = []
        for g, s in enumerate(scores):
            s0 = s[:, :A_KV_WIDTH]
            s1 = jnp.where(hi_half, sink_ref[g * 4 * CHUNK:(g + 1) * 4 * CHUNK, :],
                           s[:, A_KV_WIDTH:])
            if c < A_WIN_CHUNKS:
                n_masked = jnp.where(t == 0, (A_WIN_CHUNKS - c) * CHUNK, 0)
                s0 = jnp.where(kidx < n_masked, NEG_INF, s0)
            m = jnp.max(jnp.maximum(s0, s1), axis=-1, keepdims=True)
            out.append(jnp.concatenate([jnp.exp2(s0 - m), jnp.exp2(s1 - m)], axis=1).astype(BF16))
        return out

    def swa_values(c, probs):
        r0 = c * CHUNK
        return [_dot(p, jnp.concatenate([v_ref[g, r0:r0 + A_BAND, :], pad_v], axis=0))
                for g, p in enumerate(probs)]

    def swa_finish(c, pvs):
        r0 = c * CHUNK
        cols = []
        for pv in pvs:
            o = pv[:, :A_KV_WIDTH] / pv[:, A_KV_WIDTH:]
            cols += [jnp.where(lo, o[0:CHUNK], o[CHUNK:2 * CHUNK]),
                     jnp.where(lo, o[2 * CHUNK:3 * CHUNK], o[3 * CHUNK:4 * CHUNK])]
        ya = jnp.concatenate(cols, axis=1)
        ag = h_ref[r0:r0 + CHUNK, OFF_AG:OFF_AG + A_WIDTH]
        y_ref[r0:r0 + CHUNK, 0:A_WIDTH] = (ya * _silu(ag)).astype(BF16)

    def ret_scores(rows, h):
        cos = cos_ref[rows, :]
        sin = sin_ref[rows, :]
        q = h_ref[rows, OFF_BQ + h * B_DIM:OFF_BQ + (h + 1) * B_DIM]
        k = h_ref[rows, OFF_BK + h * B_DIM:OFF_BK + (h + 1) * B_DIM]
        qr = q * cos + pltpu.roll(q, B_DIM // 2, 1) * sin
        kr = k * cos + pltpu.roll(k, B_DIM // 2, 1) * sin
        return qr, kr, _dot_nt(qr.astype(BF16), kr.astype(BF16))

    def ret_output(rows, h, qr, sc):
        v = h_ref[rows, OFF_BV + h * B_DIM:OFF_BV + (h + 1) * B_DIM].astype(BF16)
        lhs = jnp.concatenate([(sc * dmat_ref[h]).astype(BF16), (qr * qdec_ref[h]).astype(BF16)], axis=1)
        rhs = jnp.concatenate([v, state_ref[h].astype(BF16)], axis=0)
        return _dot(lhs, rhs)

    def ret_state(rows, h, kr):
        v = h_ref[rows, OFF_BV + h * B_DIM:OFF_BV + (h + 1) * B_DIM].astype(BF16)
        kd = (kr * kdec_ref[h]).T.astype(BF16)
        state_ref[h] = state_ref[h] * gdec_ref[h] + _dot(kd, v)

    def ret_finish(rows, h, o):
        o = o * lax.rsqrt(jnp.mean(o * o, axis=-1, keepdims=True) + EPS)
        bg = h_ref[rows, OFF_BG + h * B_DIM:OFF_BG + (h + 1) * B_DIM]
        y_ref[rows, A_WIDTH + h * B_DIM:A_WIDTH + (h + 1) * B_DIM] = (o * _silu(bg)).astype(BF16)

    def mix_block(rows):
        chunks = range(rows.start // CHUNK, rows.stop // CHUNK)
        scores = [swa_scores(c) for c in chunks]
        probs = [swa_probs(c, s) for c, s in zip(chunks, scores)]
        pvs = [swa_values(c, p) for c, p in zip(chunks, probs)]
        for c, pv in zip(chunks, pvs):
            swa_finish(c, pv)
        for g0 in range(rows.start, rows.stop, B_GROUP):
            grp = slice(g0, g0 + B_GROUP)
            qks = [ret_scores(grp, h) for h in range(B_HEADS)]
            outs = [ret_output(grp, h, qr, sc) for h, (qr, _, sc) in enumerate(qks)]
            for h, (_, kr, _) in enumerate(qks):
                ret_state(grp, h, kr)
            for h, o in enumerate(outs):
                ret_finish(grp, h, o)

    blocks = [slice(s0, s0 + SUB0) for s0 in range(0, TS, SUB0)]
    xs = [x_ref[0, rows, :] for rows in blocks]
    for rows, x in zip(blocks, xs):
        xn = _rmsnorm(x, enorm_ref[...]).astype(BF16)
        h_ref[rows, :] = _dot(xn, win_ref[...])
        store_kv(rows)
    x1s = []
    for rows, x in zip(blocks, xs):
        mix_block(rows)
        x1s.append(x + _dot(y_ref[rows, :], wout_ref[...]))
    x2s = _cross_attention_blocks(x1s, cnorm_ref, mq_ref, mo_ref)
    for rows, x2 in zip(blocks, x2s):
        o_ref[0, rows, :] = x2


def _layer1_kernel(x_ref, onorm_ref, win_ref, convw_ref, convb_ref, wout_ref,
                   cnorm_ref, mq_ref, mo_ref, fnorm_ref,
                   o_ref, h_ref, z_ref):
    t = pl.program_id(1)

    @pl.when(t == 0)
    def _():
        z_ref[0:C_PAD, :] = jnp.zeros((C_PAD, D_MODEL), F32)

    @pl.when(t > 0)
    def _():
        z_ref[0:C_PAD, :] = z_ref[TS:TS + C_PAD, :]

    blocks = [slice(s0, s0 + SUB1) for s0 in range(0, TS, SUB1)]
    xs = [x_ref[0, rows, :] for rows in blocks]
    for rows, x in zip(blocks, xs):
        xn = _rmsnorm(x, onorm_ref[...]).astype(BF16)
        h_ref[rows, :] = _dot(xn, win_ref[...])
    x1s = []
    for rows, x in zip(blocks, xs):
        zrows = lambda back: slice(C_PAD - back + rows.start, C_PAD - back + rows.stop)
        z = h_ref[rows, D_MODEL:2 * D_MODEL] * h_ref[rows, 2 * D_MODEL:3 * D_MODEL]
        z_ref[zrows(0), :] = z
        conv = convb_ref[...] + z * convw_ref[2:3, :]
        conv = conv + z_ref[zrows(1), :] * convw_ref[1:2, :]
        conv = conv + z_ref[zrows(2), :] * convw_ref[0:1, :]
        y = h_ref[rows, 0:D_MODEL] * conv * _silu(h_ref[rows, 3 * D_MODEL:4 * D_MODEL])
        x1s.append(x + _dot(y.astype(BF16), wout_ref[...]))
    x2s = _cross_attention_blocks(x1s, cnorm_ref, mq_ref, mo_ref)
    for rows, x2 in zip(blocks, x2s):
        o_ref[0, rows, :] = _rmsnorm(x2, fnorm_ref[...])


def _const_spec(shape):
    return pl.BlockSpec(shape, lambda *_: (0,) * len(shape), pipeline_mode=pl.Buffered(1))


def _row_spec(layer, n_rows):
    return pl.BlockSpec((None, n_rows, D_MODEL), lambda *_: (layer, 0, 0), pipeline_mode=pl.Buffered(1))


def _retention_tables(seq):
    f32 = np.float32
    half = B_DIM // 2
    inv = (1.0 / (f32(ROT_BASE) ** np.linspace(0.0, 1.0, half, dtype=f32))).astype(f32)
    ang = np.arange(seq, dtype=f32)[:, None] * inv[None, :]
    cos, sin = np.cos(ang), np.sin(ang)
    cos2 = np.concatenate([cos, cos], axis=1)
    sin2 = np.concatenate([-sin, sin], axis=1)
    log_gamma = np.log1p(-np.exp2(-5.0 - np.arange(B_HEADS, dtype=f32))).astype(f32)
    idx = np.arange(B_GROUP, dtype=f32)
    chunk_of = np.arange(B_GROUP) // CHUNK
    visible = (chunk_of[None, :] <= chunk_of[:, None]).astype(f32)
    scale = f32(B_DIM ** -0.5)
    dmat = np.exp(log_gamma[:, None, None] * np.abs(idx[:, None] - idx[None, :])) * visible * scale
    qdec = np.exp(log_gamma[:, None] * (idx + 1.0)[None, :])
    kdec = np.exp(log_gamma[:, None] * (B_GROUP - 1 - idx)[None, :]) * scale
    qdec = np.broadcast_to(qdec[:, :, None], (B_HEADS, B_GROUP, B_DIM))
    kdec = np.broadcast_to(kdec[:, :, None], (B_HEADS, B_GROUP, B_DIM))
    gdec = np.broadcast_to(np.exp(log_gamma * B_GROUP)[:, None, None], (B_HEADS, 1, B_DIM))
    return tuple(np.ascontiguousarray(a, dtype=f32) for a in (cos2, sin2, dmat, qdec, kdec, gdec))


def _cast_specs(n_steps, step_of, layers, widths):
    rows = D_MODEL // n_steps
    assert rows * n_steps == D_MODEL and rows % 16 == 0
    ins = [pl.BlockSpec((None, rows, w), lambda *g, l=l: (l, step_of(*g), 0))
           for w, l in zip(widths, layers)]
    outs = [pl.BlockSpec((rows, w), lambda *g: (step_of(*g), 0)) for w in widths]
    shapes = [jax.ShapeDtypeStruct((D_MODEL, w), BF16) for w in widths]
    return ins, outs, shapes


def _forward(x, mem, e_norm, e_w_in, e_sink, e_w_out, o_norm, o_w_in, o_conv_w, o_conv_b,
             o_w_out, c_norm, c_mem_norm, c_wq, c_wkv, c_wo, final_norm):
    batch, seq, d = x.shape
    depth = c_wq.shape[0]
    assert d == D_MODEL and depth == 2 and seq % TS == 0 and mem.shape == (batch, MEM_LEN, D_MODEL)
    n_t = seq // TS
    params = pltpu.CompilerParams(dimension_semantics=("arbitrary", "arbitrary"),
                                  vmem_limit_bytes=VMEM_LIMIT_BYTES)
    rows3 = lambda p: p.reshape(p.shape[0], 1, D_MODEL)

    xw = X_HEADS * MEM_LEN
    layer_spec = lambda shape, **kw: pl.BlockSpec((None,) + shape, lambda i, b: (i, 0, 0), **kw)
    c_in, c_out, c_shape = _cast_specs(depth * batch, lambda i, b: i * batch + b, (0, 0),
                                       (EVEN_IN, D_MODEL))
    mq, mo, w_in0, w_out0 = pl.pallas_call(
        _memkv_kernel,
        grid=(depth, batch),
        in_specs=[pl.BlockSpec((1, MEM_LEN, D_MODEL), lambda i, b: (b, 0, 0)),
                  pl.BlockSpec((1, 1, D_MODEL), lambda i, b: (i, 0, 0)),
                  layer_spec((D_MODEL, 2 * D_MODEL)),
                  layer_spec((D_MODEL, D_MODEL)),
                  layer_spec((D_MODEL, D_MODEL))] + c_in,
        out_specs=[pl.BlockSpec((1, 1, D_MODEL, xw), lambda i, b: (i, b, 0, 0)),
                   pl.BlockSpec((1, 1, xw, D_MODEL), lambda i, b: (i, b, 0, 0))] + c_out,
        out_shape=[jax.ShapeDtypeStruct((depth, batch, D_MODEL, xw), BF16),
                   jax.ShapeDtypeStruct((depth, batch, xw, D_MODEL), BF16)] + c_shape,
        scratch_shapes=[pltpu.VMEM((D_MODEL, 2 * D_MODEL), BF16),
                        pltpu.VMEM((D_MODEL, D_MODEL), BF16),
                        pltpu.VMEM((D_MODEL, D_MODEL), BF16)],
        compiler_params=pltpu.CompilerParams(dimension_semantics=("arbitrary", "arbitrary"),
                                             vmem_limit_bytes=MEMKV_VMEM_LIMIT_BYTES),
        name="memkv",
    )(mem, c_mem_norm.reshape(depth, 1, D_MODEL), c_wkv, c_wq, c_wo, e_w_in, e_w_out)

    cos2, sin2, dmat, qdec, kdec, gdec = _retention_tables(seq)
    sink_rows = jnp.broadcast_to((e_sink[0].astype(F32) * LOG2E)[:, None, None],
                                 (A_Q_HEADS, CHUNK, A_KV_WIDTH)).reshape(A_Q_HEADS * CHUNK, A_KV_WIDTH)
    sink_pad = jnp.where(np.arange(A_KV_WIDTH)[None, :] == A_BAND - A_KV_WIDTH,
                         sink_rows, NEG_INF).astype(F32)

    tile_spec = pl.BlockSpec((1, TS, D_MODEL), lambda b, t: (b, t, 0))
    mq_spec = lambda i: pl.BlockSpec((None, 1, D_MODEL, xw), lambda b, t: (i, b, 0, 0))
    mo_spec = lambda i: pl.BlockSpec((None, 1, xw, D_MODEL), lambda b, t: (i, b, 0, 0))
    c_in, c_out, c_shape = _cast_specs(batch * n_t, lambda b, t: b * n_t + t, (0, 0),
                                       (4 * D_MODEL, D_MODEL))

    x1, w_in1, w_out1 = pl.pallas_call(
        _layer0_kernel,
        grid=(batch, n_t),
        in_specs=[tile_spec,
                  pl.BlockSpec((TS, B_DIM), lambda b, t: (t, 0)),
                  pl.BlockSpec((TS, B_DIM), lambda b, t: (t, 0)),
                  _const_spec((B_HEADS, B_GROUP, B_GROUP)),
                  _const_spec((B_HEADS, B_GROUP, B_DIM)),
                  _const_spec((B_HEADS, B_GROUP, B_DIM)),
                  _const_spec((B_HEADS, 1, B_DIM)),
                  _const_spec((A_Q_HEADS * CHUNK, A_KV_WIDTH)),
                  _row_spec(0, 1),
                  _const_spec((D_MODEL, EVEN_IN)),
                  _const_spec((MIX_WIDTH, D_MODEL)),
                  _row_spec(0, 1),
                  mq_spec(0), mo_spec(0)] + c_in,
        out_specs=[tile_spec] + c_out,
        out_shape=[jax.ShapeDtypeStruct((batch, seq, D_MODEL), F32)] + c_shape,
        scratch_shapes=[pltpu.VMEM((TS, EVEN_IN), F32),
                        pltpu.VMEM((TS, MIX_WIDTH), BF16),
                        pltpu.VMEM((A_KV_HEADS, A_TAIL + TS, A_KV_WIDTH), BF16),
                        pltpu.VMEM((A_KV_HEADS, A_TAIL + TS, 2 * A_KV_WIDTH), BF16),
                        pltpu.VMEM((B_HEADS, B_DIM, B_DIM), F32)],
        compiler_params=params,
        name="layer0",
    )(x, cos2, sin2, dmat, qdec, kdec, gdec, sink_pad,
      rows3(e_norm), w_in0, w_out0,
      rows3(c_norm), mq, mo,
      o_w_in, o_w_out)

    out = pl.pallas_call(
        _layer1_kernel,
        grid=(batch, n_t),
        in_specs=[tile_spec,
                  _row_spec(0, 1),
                  _const_spec((D_MODEL, 4 * D_MODEL)),
                  _row_spec(0, C_CONV),
                  _row_spec(0, 1),
                  _const_spec((D_MODEL, D_MODEL)),
                  _row_spec(1, 1),
                  mq_spec(1), mo_spec(1),
                  _const_spec((1, D_MODEL))],
        out_specs=tile_spec,
        out_shape=jax.ShapeDtypeStruct((batch, seq, D_MODEL), F32),
        scratch_shapes=[pltpu.VMEM((TS, 4 * D_MODEL), F32),
                        pltpu.VMEM((C_PAD + TS, D_MODEL), F32)],
        compiler_params=params,
        name="layer1",
    )(x1, rows3(o_norm), w_in1, o_conv_w,
      rows3(o_conv_b), w_out1,
      rows3(c_norm), mq, mo, final_norm.reshape(1, D_MODEL))
    return x1, out


def kernel(x, mem, e_norm, e_w_in, e_sink, e_w_out, o_norm, o_w_in, o_conv_w, o_conv_b,
           o_w_out, c_norm, c_mem_norm, c_wq, c_wkv, c_wo, final_norm):
    return _forward(x, mem, e_norm, e_w_in, e_sink, e_w_out, o_norm, o_w_in, o_conv_w, o_conv_b,
                    o_w_out, c_norm, c_mem_norm, c_wq, c_wkv, c_wo, final_norm)[1]
```

```python
import jax
import jax.numpy as jnp
import numpy as np
from jax import lax
from jax.experimental import pallas as pl
from jax.experimental.pallas import tpu as pltpu

F32 = jnp.float32
BF16 = jnp.bfloat16

D_MODEL = 1024
CHUNK = 64
MEM_LEN = 256
EPS = 1e-6
NEG_INF = -1e30
LOG2E = 1.4426950408889634

A_Q_HEADS = 8
A_KV_HEADS = 2
A_HEAD_DIM = 64
A_WIN_CHUNKS = 2
A_WIDTH = A_Q_HEADS * A_HEAD_DIM
A_KV_WIDTH = A_KV_HEADS * A_HEAD_DIM
A_BAND = (A_WIN_CHUNKS + 1) * CHUNK
A_TAIL = A_WIN_CHUNKS * CHUNK
A_SCALE = A_HEAD_DIM ** -0.5 * LOG2E

B_HEADS = 4
B_DIM = 128
B_WIDTH = B_HEADS * B_DIM
ROT_BASE = 10000.0
B_GROUP = 256

C_CONV = 3
C_PAD = 8

X_HEADS = 4
X_DIM = D_MODEL // X_HEADS
X_SCALE = X_DIM ** -0.5 * LOG2E

OFF_AQ = 0
OFF_AK = OFF_AQ + A_WIDTH
OFF_AV = OFF_AK + A_KV_WIDTH
OFF_AG = OFF_AV + A_KV_WIDTH
OFF_BQ = OFF_AG + A_WIDTH
OFF_BK = OFF_BQ + B_WIDTH
OFF_BV = OFF_BK + B_WIDTH
OFF_BG = OFF_BV + B_WIDTH
EVEN_IN = OFF_BG + B_WIDTH
MIX_WIDTH = A_WIDTH + B_WIDTH

TS = 1024
SUB0 = 256
SUB1 = 256
VMEM_LIMIT_BYTES = 56 * 1024 * 1024
MEMKV_VMEM_LIMIT_BYTES = 60 * 1024 * 1024


def _rmsnorm(x, g):
    return x * lax.rsqrt(jnp.mean(x * x, axis=-1, keepdims=True) + EPS) * g


def _silu(x):
    return x * jax.nn.sigmoid(x)


def _dot(a, b):
    return jnp.dot(a, b, preferred_element_type=F32)


def _dot_nt(a, b):
    return lax.dot_general(a, b, (((1,), (1,)), ((), ())), preferred_element_type=F32)


def _cast_blocks(pairs):
    for src_ref, dst_ref in pairs:
        dst_ref[...] = src_ref[...].astype(BF16)


def _memkv_kernel(mem_ref, g_ref, wkv_ref, wq_ref, wo_ref, win_ref, wout_ref,
                  mq_ref, mo_ref, win_bf_ref, wout_bf_ref, wkv_bf_ref, wq_bf_ref, wo_bf_ref):
    @pl.when(pl.program_id(1) == 0)
    def _():
        wkv_bf_ref[...] = wkv_ref[...].astype(BF16)
        wq_bf_ref[...] = wq_ref[...].astype(BF16)
        wo_bf_ref[...] = wo_ref[...].astype(BF16)

    mn = _rmsnorm(mem_ref[0], g_ref[0]).astype(BF16)
    kv = _dot(mn, wkv_bf_ref[...])
    k = (kv[:, :D_MODEL] * X_SCALE).astype(BF16)
    v = kv[:, D_MODEL:].astype(BF16)
    for h in range(X_HEADS):
        cs = slice(h * X_DIM, (h + 1) * X_DIM)
        mq_ref[0, 0, :, h * MEM_LEN:(h + 1) * MEM_LEN] = _dot_nt(wq_bf_ref[:, cs], k[:, cs]).astype(BF16)
        mo_ref[0, 0, h * MEM_LEN:(h + 1) * MEM_LEN, :] = _dot(v[:, cs], wo_bf_ref[cs, :]).astype(BF16)
    _cast_blocks([(win_ref, win_bf_ref), (wout_ref, wout_bf_ref)])


def _softmax_heads(sc):
    heads = []
    for h in range(X_HEADS):
        s = sc[:, h * MEM_LEN:(h + 1) * MEM_LEN]
        p = jnp.exp2(s - jnp.max(s, axis=-1, keepdims=True))
        heads.append((p * (1.0 / jnp.sum(p, axis=-1, keepdims=True))).astype(BF16))
    return jnp.concatenate(heads, axis=1)


def _cross_attention_blocks(x1s, cnorm_ref, mq_ref, mo_ref):
    scores = [_dot(_rmsnorm(x1, cnorm_ref[...]).astype(BF16), mq_ref[0]) for x1 in x1s]
    probs = [_softmax_heads(sc) for sc in scores]
    return [x1 + _dot(p, mo_ref[0]) for x1, p in zip(x1s, probs)]


def _layer0_kernel(x_ref, cos_ref, sin_ref, dmat_ref, qdec_ref, kdec_ref, gdec_ref, sink_ref,
                   enorm_ref, win_ref, wout_ref, cnorm_ref, mq_ref, mo_ref,
                   nwin_ref, nwout_ref,
                   o_ref, nwin_bf_ref, nwout_bf_ref,
                   h_ref, y_ref, k_ref, v_ref, state_ref):
    t = pl.program_id(1)
    _cast_blocks([(nwin_ref, nwin_bf_ref), (nwout_ref, nwout_bf_ref)])

    @pl.when(t == 0)
    def _():
        k_ref[:, 0:A_TAIL, :] = jnp.zeros((A_KV_HEADS, A_TAIL, A_KV_WIDTH), BF16)
        v_ref[:, 0:A_TAIL, 0:A_KV_WIDTH] = jnp.zeros((A_KV_HEADS, A_TAIL, A_KV_WIDTH), BF16)
        v_ref[:, :, A_KV_WIDTH:] = jnp.ones((A_KV_HEADS, A_TAIL + TS, A_KV_WIDTH), BF16)
        state_ref[...] = jnp.zeros_like(state_ref)

    @pl.when(t > 0)
    def _():
        k_ref[:, 0:A_TAIL, :] = k_ref[:, TS:TS + A_TAIL, :]
        v_ref[:, 0:A_TAIL, 0:A_KV_WIDTH] = v_ref[:, TS:TS + A_TAIL, 0:A_KV_WIDTH]

    lo = lax.broadcasted_iota(jnp.int32, (CHUNK, A_KV_WIDTH), 1) < A_HEAD_DIM
    lo_blk = lax.broadcasted_iota(jnp.int32, (SUB0, A_KV_WIDTH), 1) < A_HEAD_DIM
    kidx = lax.broadcasted_iota(jnp.int32, (A_KV_WIDTH, 4 * CHUNK), 0)
    hi_half = kidx >= A_BAND - A_KV_WIDTH
    pad_k = jnp.zeros((2 * A_KV_WIDTH - A_BAND, A_KV_WIDTH), BF16)
    pad_r = lax.broadcasted_iota(jnp.int32, (2 * A_KV_WIDTH - A_BAND, 2 * A_KV_WIDTH), 0)
    pad_c = lax.broadcasted_iota(jnp.int32, (2 * A_KV_WIDTH - A_BAND, 2 * A_KV_WIDTH), 1)
    pad_v = jnp.where((pad_r == 0) & (pad_c >= A_KV_WIDTH), 1.0, 0.0).astype(BF16)

    def store_kv(rows):
        hist = slice(A_TAIL + rows.start, A_TAIL + rows.stop)
        kk = h_ref[rows, OFF_AK:OFF_AK + A_KV_WIDTH] * A_SCALE
        vv = h_ref[rows, OFF_AV:OFF_AV + A_KV_WIDTH]
        kr = pltpu.roll(kk, A_HEAD_DIM, 1)
        vr = pltpu.roll(vv, A_HEAD_DIM, 1)
        k_ref[0, hist, :] = jnp.where(lo_blk, kk, kr).astype(BF16)
        k_ref[1, hist, :] = jnp.where(lo_blk, kr, kk).astype(BF16)
        v_ref[0, hist, 0:A_KV_WIDTH] = jnp.where(lo_blk, vv, vr).astype(BF16)
        v_ref[1, hist, 0:A_KV_WIDTH] = jnp.where(lo_blk, vr, vv).astype(BF16)

    def swa_scores(c):
        r0 = c * CHUNK
        q = h_ref[r0:r0 + CHUNK, OFF_AQ:OFF_AQ + A_WIDTH]
        out = []
        for g in range(A_KV_HEADS):
            w = 2 * A_HEAD_DIM
            qa, qb = q[:, (2 * g) * w:(2 * g + 1) * w], q[:, (2 * g + 1) * w:(2 * g + 2) * w]
            zero = jnp.zeros_like(qa)
            lhs = jnp.concatenate([jnp.where(lo, qa, zero), jnp.where(lo, zero, qa),
                                   jnp.where(lo, qb, zero), jnp.where(lo, zero, qb)],
                                  axis=0).astype(BF16)
            out.append(_dot_nt(jnp.concatenate([k_ref[g, r0:r0 + A_BAND, :], pad_k], axis=0), lhs))
        return out

    def swa_probs(c, scores):
        out = []
        for g, s in enumerate(scores):
            s0 = s[:A_KV_WIDTH, :]
            s1 = jnp.where(hi_half, sink_ref[:, g * 4 * CHUNK:(g + 1) * 4 * CHUNK],
                           s[A_KV_WIDTH:, :])
            if c < A_WIN_CHUNKS:
                n_masked = jnp.where(t == 0, (A_WIN_CHUNKS - c) * CHUNK, 0)
                s0 = jnp.where(kidx < n_masked, NEG_INF, s0)
            m = jnp.max(jnp.maximum(s0, s1), axis=0, keepdims=True)
            out.append(jnp.concatenate([jnp.exp2(s0 - m), jnp.exp2(s1 - m)], axis=0).astype(BF16))
        return out

    def swa_values(c, probs):
        r0 = c * CHUNK
        return [lax.dot_general(p, jnp.concatenate([v_ref[g, r0:r0 + A_BAND, :], pad_v], axis=0),
                                (((0,), (0,)), ((), ())), preferred_element_type=F32)
                for g, p in enumerate(probs)]

    def swa_finish(c, pvs):
        r0 = c * CHUNK
        cols = []
        for pv in pvs:
            o = pv[:, :A_KV_WIDTH] / pv[:, A_KV_WIDTH:]
            cols += [jnp.where(lo, o[0:CHUNK], o[CHUNK:2 * CHUNK]),
                     jnp.where(lo, o[2 * CHUNK:3 * CHUNK], o[3 * CHUNK:4 * CHUNK])]
        ya = jnp.concatenate(cols, axis=1)
        ag = h_ref[r0:r0 + CHUNK, OFF_AG:OFF_AG + A_WIDTH]
        y_ref[r0:r0 + CHUNK, 0:A_WIDTH] = (ya * _silu(ag)).astype(BF16)

    def ret_scores(rows, h):
        cos = cos_ref[rows, :]
        sin = sin_ref[rows, :]
        q = h_ref[rows, OFF_BQ + h * B_DIM:OFF_BQ + (h + 1) * B_DIM]
        k = h_ref[rows, OFF_BK + h * B_DIM:OFF_BK + (h + 1) * B_DIM]
        qr = q * cos + pltpu.roll(q, B_DIM // 2, 1) * sin
        kr = k * cos + pltpu.roll(k, B_DIM // 2, 1) * sin
        return qr, kr, _dot_nt(qr.astype(BF16), kr.astype(BF16))

    def ret_output(rows, h, qr, sc):
        v = h_ref[rows, OFF_BV + h * B_DIM:OFF_BV + (h + 1) * B_DIM].astype(BF16)
        lhs = jnp.concatenate([(sc * dmat_ref[h]).astype(BF16), (qr * qdec_ref[h]).astype(BF16)], axis=1)
        rhs = jnp.concatenate([v, state_ref[h].astype(BF16)], axis=0)
        return _dot(lhs, rhs)

    def ret_state(rows, h, kr):
        v = h_ref[rows, OFF_BV + h * B_DIM:OFF_BV + (h + 1) * B_DIM].astype(BF16)
        kd = (kr * kdec_ref[h]).T.astype(BF16)
        state_ref[h] = state_ref[h] * gdec_ref[h] + _dot(kd, v)

    def ret_finish(rows, h, o):
        o = o * lax.rsqrt(jnp.mean(o * o, axis=-1, keepdims=True) + EPS)
        bg = h_ref[rows, OFF_BG + h * B_DIM:OFF_BG + (h + 1) * B_DIM]
        y_ref[rows, A_WIDTH + h * B_DIM:A_WIDTH + (h + 1) * B_DIM] = (o * _silu(bg)).astype(BF16)

    def mix_block(rows):
        chunks = range(rows.start // CHUNK, rows.stop // CHUNK)
        scores = [swa_scores(c) for c in chunks]
        probs = [swa_probs(c, s) for c, s in zip(chunks, scores)]
        pvs = [swa_values(c, p) for c, p in zip(chunks, probs)]
        for c, pv in zip(chunks, pvs):
            swa_finish(c, pv)
        for g0 in range(rows.start, rows.stop, B_GROUP):
            grp = slice(g0, g0 + B_GROUP)
            qks = [ret_scores(grp, h) for h in range(B_HEADS)]
            outs = [ret_output(grp, h, qr, sc) for h, (qr, _, sc) in enumerate(qks)]
            for h, (_, kr, _) in enumerate(qks):
                ret_state(grp, h, kr)
            for h, o in enumerate(outs):
                ret_finish(grp, h, o)

    blocks = [slice(s0, s0 + SUB0) for s0 in range(0, TS, SUB0)]
    xs = [x_ref[0, rows, :] for rows in blocks]
    for rows, x in zip(blocks, xs):
        xn = _rmsnorm(x, enorm_ref[...]).astype(BF16)
        h_ref[rows, :] = _dot(xn, win_ref[...])
        store_kv(rows)
    x1s = []
    for rows, x in zip(blocks, xs):
        mix_block(rows)
        x1s.append(x + _dot(y_ref[rows, :], wout_ref[...]))
    x2s = _cross_attention_blocks(x1s, cnorm_ref, mq_ref, mo_ref)
    for rows, x2 in zip(blocks, x2s):
        o_ref[0, rows, :] = x2


def _layer1_kernel(x_ref, onorm_ref, win_ref, convw_ref, convb_ref, wout_ref,
                   cnorm_ref, mq_ref, mo_ref, fnorm_ref,
                   o_ref, h_ref, z_ref):
    t = pl.program_id(1)

    @pl.when(t == 0)
    def _():
        z_ref[0:C_PAD, :] = jnp.zeros((C_PAD, D_MODEL), F32)

    @pl.when(t > 0)
    def _():
        z_ref[0:C_PAD, :] = z_ref[TS:TS + C_PAD, :]

    blocks = [slice(s0, s0 + SUB1) for s0 in range(0, TS, SUB1)]
    xs = [x_ref[0, rows, :] for rows in blocks]
    for rows, x in zip(blocks, xs):
        xn = _rmsnorm(x, onorm_ref[...]).astype(BF16)
        h_ref[rows, :] = _dot(xn, win_ref[...])
    x1s = []
    for rows, x in zip(blocks, xs):
        zrows = lambda back: slice(C_PAD - back + rows.start, C_PAD - back + rows.stop)
        z = h_ref[rows, D_MODEL:2 * D_MODEL] * h_ref[rows, 2 * D_MODEL:3 * D_MODEL]
        z_ref[zrows(0), :] = z
        conv = convb_ref[...] + z * convw_ref[2:3, :]
        conv = conv + z_ref[zrows(1), :] * convw_ref[1:2, :]
        conv = conv + z_ref[zrows(2), :] * convw_ref[0:1, :]
        y = h_ref[rows, 0:D_MODEL] * conv * _silu(h_ref[rows, 3 * D_MODEL:4 * D_MODEL])
        x1s.append(x + _dot(y.astype(BF16), wout_ref[...]))
    x2s = _cross_attention_blocks(x1s, cnorm_ref, mq_ref, mo_ref)
    for rows, x2 in zip(blocks, x2s):
        o_ref[0, rows, :] = _rmsnorm(x2, fnorm_ref[...])


def _const_spec(shape):
    return pl.BlockSpec(shape, lambda *_: (0,) * len(shape), pipeline_mode=pl.Buffered(1))


def _row_spec(layer, n_rows):
    return pl.BlockSpec((None, n_rows, D_MODEL), lambda *_: (layer, 0, 0), pipeline_mode=pl.Buffered(1))


def _retention_tables(seq):
    f32 = np.float32
    half = B_DIM // 2
    inv = (1.0 / (f32(ROT_BASE) ** np.linspace(0.0, 1.0, half, dtype=f32))).astype(f32)
    ang = np.arange(seq, dtype=f32)[:, None] * inv[None, :]
    cos, sin = np.cos(ang), np.sin(ang)
    cos2 = np.concatenate([cos, cos], axis=1)
    sin2 = np.concatenate([-sin, sin], axis=1)
    log_gamma = np.log1p(-np.exp2(-5.0 - np.arange(B_HEADS, dtype=f32))).astype(f32)
    idx = np.arange(B_GROUP, dtype=f32)
    chunk_of = np.arange(B_GROUP) // CHUNK
    visible = (chunk_of[None, :] <= chunk_of[:, None]).astype(f32)
    scale = f32(B_DIM ** -0.5)
    dmat = np.exp(log_gamma[:, None, None] * np.abs(idx[:, None] - idx[None, :])) * visible * scale
    qdec = np.exp(log_gamma[:, None] * (idx + 1.0)[None, :])
    kdec = np.exp(log_gamma[:, None] * (B_GROUP - 1 - idx)[None, :]) * scale
    qdec = np.broadcast_to(qdec[:, :, None], (B_HEADS, B_GROUP, B_DIM))
    kdec = np.broadcast_to(kdec[:, :, None], (B_HEADS, B_GROUP, B_DIM))
    gdec = np.broadcast_to(np.exp(log_gamma * B_GROUP)[:, None, None], (B_HEADS, 1, B_DIM))
    return tuple(np.ascontiguousarray(a, dtype=f32) for a in (cos2, sin2, dmat, qdec, kdec, gdec))


def _cast_specs(n_steps, step_of, layers, widths):
    rows = D_MODEL // n_steps
    assert rows * n_steps == D_MODEL and rows % 16 == 0
    ins = [pl.BlockSpec((None, rows, w), lambda *g, l=l: (l, step_of(*g), 0))
           for w, l in zip(widths, layers)]
    outs = [pl.BlockSpec((rows, w), lambda *g: (step_of(*g), 0)) for w in widths]
    shapes = [jax.ShapeDtypeStruct((D_MODEL, w), BF16) for w in widths]
    return ins, outs, shapes


def _forward(x, mem, e_norm, e_w_in, e_sink, e_w_out, o_norm, o_w_in, o_conv_w, o_conv_b,
             o_w_out, c_norm, c_mem_norm, c_wq, c_wkv, c_wo, final_norm):
    batch, seq, d = x.shape
    depth = c_wq.shape[0]
    assert d == D_MODEL and depth == 2 and seq % TS == 0 and mem.shape == (batch, MEM_LEN, D_MODEL)
    n_t = seq // TS
    params = pltpu.CompilerParams(dimension_semantics=("arbitrary", "arbitrary"),
                                  vmem_limit_bytes=VMEM_LIMIT_BYTES)
    rows3 = lambda p: p.reshape(p.shape[0], 1, D_MODEL)

    xw = X_HEADS * MEM_LEN
    layer_spec = lambda shape, **kw: pl.BlockSpec((None,) + shape, lambda i, b: (i, 0, 0), **kw)
    c_in, c_out, c_shape = _cast_specs(depth * batch, lambda i, b: i * batch + b, (0, 0),
                                       (EVEN_IN, D_MODEL))
    mq, mo, w_in0, w_out0 = pl.pallas_call(
        _memkv_kernel,
        grid=(depth, batch),
        in_specs=[pl.BlockSpec((1, MEM_LEN, D_MODEL), lambda i, b: (b, 0, 0)),
                  pl.BlockSpec((1, 1, D_MODEL), lambda i, b: (i, 0, 0)),
                  layer_spec((D_MODEL, 2 * D_MODEL)),
                  layer_spec((D_MODEL, D_MODEL)),
                  layer_spec((D_MODEL, D_MODEL))] + c_in,
        out_specs=[pl.BlockSpec((1, 1, D_MODEL, xw), lambda i, b: (i, b, 0, 0)),
                   pl.BlockSpec((1, 1, xw, D_MODEL), lambda i, b: (i, b, 0, 0))] + c_out,
        out_shape=[jax.ShapeDtypeStruct((depth, batch, D_MODEL, xw), BF16),
                   jax.ShapeDtypeStruct((depth, batch, xw, D_MODEL), BF16)] + c_shape,
        scratch_shapes=[pltpu.VMEM((D_MODEL, 2 * D_MODEL), BF16),
                        pltpu.VMEM((D_MODEL, D_MODEL), BF16),
                        pltpu.VMEM((D_MODEL, D_MODEL), BF16)],
        compiler_params=pltpu.CompilerParams(dimension_semantics=("arbitrary", "arbitrary"),
                                             vmem_limit_bytes=MEMKV_VMEM_LIMIT_BYTES),
        name="memkv",
    )(mem, c_mem_norm.reshape(depth, 1, D_MODEL), c_wkv, c_wq, c_wo, e_w_in, e_w_out)

    cos2, sin2, dmat, qdec, kdec, gdec = _retention_tables(seq)
    sink_cols = jnp.broadcast_to((e_sink[0].astype(F32) * LOG2E)[None, :, None],
                                 (A_KV_WIDTH, A_Q_HEADS, CHUNK)).reshape(A_KV_WIDTH, A_Q_HEADS * CHUNK)
    sink_pad = jnp.where(np.arange(A_KV_WIDTH)[:, None] == A_BAND - A_KV_WIDTH,
                         sink_cols, NEG_INF).astype(F32)

    tile_spec = pl.BlockSpec((1, TS, D_MODEL), lambda b, t: (b, t, 0))
    mq_spec = lambda i: pl.BlockSpec((None, 1, D_MODEL, xw), lambda b, t: (i, b, 0, 0))
    mo_spec = lambda i: pl.BlockSpec((None, 1, xw, D_MODEL), lambda b, t: (i, b, 0, 0))
    c_in, c_out, c_shape = _cast_specs(batch * n_t, lambda b, t: b * n_t + t, (0, 0),
                                       (4 * D_MODEL, D_MODEL))

    x1, w_in1, w_out1 = pl.pallas_call(
        _layer0_kernel,
        grid=(batch, n_t),
        in_specs=[tile_spec,
                  pl.BlockSpec((TS, B_DIM), lambda b, t: (t, 0)),
                  pl.BlockSpec((TS, B_DIM), lambda b, t: (t, 0)),
                  _const_spec((B_HEADS, B_GROUP, B_GROUP)),
                  _const_spec((B_HEADS, B_GROUP, B_DIM)),
                  _const_spec((B_HEADS, B_GROUP, B_DIM)),
                  _const_spec((B_HEADS, 1, B_DIM)),
                  _const_spec((A_KV_WIDTH, A_Q_HEADS * CHUNK)),
                  _row_spec(0, 1),
                  _const_spec((D_MODEL, EVEN_IN)),
                  _const_spec((MIX_WIDTH, D_MODEL)),
                  _row_spec(0, 1),
                  mq_spec(0), mo_spec(0)] + c_in,
        out_specs=[tile_spec] + c_out,
        out_shape=[jax.ShapeDtypeStruct((batch, seq, D_MODEL), F32)] + c_shape,
        scratch_shapes=[pltpu.VMEM((TS, EVEN_IN), F32),
                        pltpu.VMEM((TS, MIX_WIDTH), BF16),
                        pltpu.VMEM((A_KV_HEADS, A_TAIL + TS, A_KV_WIDTH), BF16),
                        pltpu.VMEM((A_KV_HEADS, A_TAIL + TS, 2 * A_KV_WIDTH), BF16),
                        pltpu.VMEM((B_HEADS, B_DIM, B_DIM), F32)],
        compiler_params=params,
        name="layer0",
    )(x, cos2, sin2, dmat, qdec, kdec, gdec, sink_pad,
      rows3(e_norm), w_in0, w_out0,
      rows3(c_norm), mq, mo,
      o_w_in, o_w_out)

    out = pl.pallas_call(
        _layer1_kernel,
        grid=(batch, n_t),
        in_specs=[tile_spec,
                  _row_spec(0, 1),
                  _const_spec((D_MODEL, 4 * D_MODEL)),
                  _row_spec(0, C_CONV),
                  _row_spec(0, 1),
                  _const_spec((D_MODEL, D_MODEL)),
                  _row_spec(1, 1),
                  mq_spec(1), mo_spec(1),
                  _const_spec((1, D_MODEL))],
        out_specs=tile_spec,
        out_shape=jax.ShapeDtypeStruct((batch, seq, D_MODEL), F32),
        scratch_shapes=[pltpu.VMEM((TS, 4 * D_MODEL), F32),
                        pltpu.VMEM((C_PAD + TS, D_MODEL), F32)],
        compiler_params=params,
        name="layer1",
    )(x1, rows3(o_norm), w_in1, o_conv_w,
      rows3(o_conv_b), w_out1,
      rows3(c_norm), mq, mo, final_norm.reshape(1, D_MODEL))
    return x1, out


def kernel(x, mem, e_norm, e_w_in, e_sink, e_w_out, o_norm, o_w_in, o_conv_w, o_conv_b,
           o_w_out, c_norm, c_mem_norm, c_wq, c_wkv, c_wo, final_norm):
    return _forward(x, mem, e_norm, e_w_in, e_sink, e_w_out, o_norm, o_w_in, o_conv_w, o_conv_b,
                    o_w_out, c_norm, c_mem_norm, c_wq, c_wkv, c_wo, final_norm)[1]
```
